```python
import jax, jax.numpy as jnp
from jax import lax
import numpy as np

D_MODEL = 1024
BATCH = 8
SEQ = 2048
DEPTH = 4

N_MIXERS = 2
CHUNK = 128
GMLP_WIDTH = 2 * D_MODEL
GMLP_GROUPS = 8
GMLP_GROUP_DIM = GMLP_WIDTH // GMLP_GROUPS
FOX_HEAD_DIM = 64
FOX_HEADS = D_MODEL // FOX_HEAD_DIM
FOX_WIDTH = FOX_HEADS * FOX_HEAD_DIM
Q_BLOCK = 128
D_FF = 4 * D_MODEL
N_GMLP = (DEPTH + 1) // 2
N_FOX = DEPTH // 2
RMS_EPS = 1e-6
LN_EPS = 1e-5

kernel_name = "hybrid_gmlp_fox_sqrelu_trunk"


def rms_norm(x, g):
    xf = x.astype(jnp.float32)
    y = xf * lax.rsqrt(jnp.mean(xf * xf, axis=-1, keepdims=True) + RMS_EPS)
    return (y * g.astype(jnp.float32)).astype(x.dtype)


def layer_norm(x, g, b):
    xf = x.astype(jnp.float32)
    mu = jnp.mean(xf, axis=-1, keepdims=True)
    xc = xf - mu
    var = jnp.mean(xc * xc, axis=-1, keepdims=True)
    y = xc * lax.rsqrt(var + LN_EPS) * g.astype(jnp.float32) + b.astype(jnp.float32)
    return y.astype(x.dtype)


def gmlp_mixer(h, w_in, ln_g, ln_b, w_s, b_s, w_out):
    B, S, _ = h.shape
    z = jax.nn.gelu(h @ w_in)
    u, v = jnp.split(z, 2, axis=-1)
    v = layer_norm(v, ln_g, ln_b)
    v = v.reshape(B, S // CHUNK, CHUNK, GMLP_GROUPS, GMLP_GROUP_DIM)
    causal = jnp.tril(jnp.ones((CHUNK, CHUNK), dtype=bool))
    w = jnp.where(causal[None], w_s, jnp.zeros((), w_s.dtype))
    s = jnp.einsum('gts,bnsgc->bntgc', w, v) + b_s.T[None, None, :, :, None]
    out = u * s.reshape(B, S, GMLP_WIDTH)
    return out @ w_out


def fox_mixer(h, w_in, b_f, q_g, k_g, w_out):
    B, S, _ = h.shape
    W, H, Dh = FOX_WIDTH, FOX_HEADS, FOX_HEAD_DIM
    p = h @ w_in
    q, k, v, gate, f_logit = jnp.split(p, [W, 2 * W, 3 * W, 4 * W], axis=-1)
    q = rms_norm(q.reshape(B, S, H, Dh), q_g).transpose(0, 2, 1, 3)
    k = rms_norm(k.reshape(B, S, H, Dh), k_g).transpose(0, 2, 1, 3)
    v = v.reshape(B, S, H, Dh).transpose(0, 2, 1, 3)
    log_f = jax.nn.log_sigmoid((f_logit + b_f).astype(jnp.float32))
    c = jnp.cumsum(log_f, axis=1).transpose(0, 2, 1)
    scale = Dh ** -0.5
    outs = []
    for i in range(S // Q_BLOCK):
        q0 = i * Q_BLOCK
        kend = q0 + Q_BLOCK
        qb = q[:, :, q0:kend]
        kb = k[:, :, :kend]
        vb = v[:, :, :kend]
        logits = jnp.einsum('bhtd,bhsd->bhts', qb, kb).astype(jnp.float32) * scale
        logits = logits + c[:, :, q0:kend, None] - c[:, :, None, :kend]
        t_idx = q0 + jnp.arange(Q_BLOCK)[:, None]
        s_idx = jnp.arange(kend)[None, :]
        logits = jnp.where(t_idx >= s_idx, logits, -jnp.inf)
        probs = jax.nn.softmax(logits, axis=-1).astype(vb.dtype)
        outs.append(jnp.einsum('bhts,bhsd->bhtd', probs, vb))
    o = jnp.concatenate(outs, axis=2).transpose(0, 2, 1, 3).reshape(B, S, W)
    o = o * jax.nn.sigmoid(gate)
    return o @ w_out


def sqrelu_mlp(h, w1, w2):
    return jnp.square(jax.nn.relu(h @ w1)) @ w2


def setup_inputs(seed: int = 0) -> dict:
    key = jax.random.key(seed)
    ks = jax.random.split(key, 20)
    f32 = jnp.float32
    D, E, G, W, H, Dh = D_MODEL, GMLP_WIDTH, GMLP_GROUPS, FOX_WIDTH, FOX_HEADS, FOX_HEAD_DIM
    nrm = lambda k, shape, s: jax.random.normal(k, shape, f32) * s
    x = nrm(ks[0], (BATCH, SEQ, D), 1.0)
    gmlp_w_in = nrm(ks[1], (N_GMLP, D, 2 * E), D ** -0.5)
    gmlp_ln_g = 1.0 + nrm(ks[2], (N_GMLP, E), 0.02)
    gmlp_ln_b = nrm(ks[3], (N_GMLP, E), 0.02)
    gmlp_w_s = nrm(ks[4], (N_GMLP, G, CHUNK, CHUNK), CHUNK ** -0.5)
    gmlp_b_s = 1.0 + nrm(ks[5], (N_GMLP, G, CHUNK), 0.1)
    gmlp_w_out = nrm(ks[6], (N_GMLP, E, D), E ** -0.5)
    fox_w_in = nrm(ks[7], (N_FOX, D, 4 * W + H), D ** -0.5)
    fox_b_f = jnp.linspace(0.0, 4.0, H, dtype=f32)[None, :] + nrm(ks[8], (N_FOX, H), 0.1)
    fox_q_g = 1.0 + nrm(ks[9], (N_FOX, Dh), 0.02)
    fox_k_g = 1.0 + nrm(ks[10], (N_FOX, Dh), 0.02)
    fox_w_out = nrm(ks[11], (N_FOX, W, D), W ** -0.5)
    mix_norm_g = 1.0 + nrm(ks[12], (DEPTH, D), 0.02)
    mlp_norm_g = 1.0 + nrm(ks[13], (DEPTH, D), 0.02)
    mlp_w1 = nrm(ks[14], (DEPTH, D, D_FF), D ** -0.5)
    mlp_w2 = nrm(ks[15], (DEPTH, D_FF, D), D_FF ** -0.5)
    return {"x": x, "gmlp_w_in": gmlp_w_in, "gmlp_ln_g": gmlp_ln_g, "gmlp_ln_b": gmlp_ln_b,
            "gmlp_w_s": gmlp_w_s, "gmlp_b_s": gmlp_b_s, "gmlp_w_out": gmlp_w_out,
            "fox_w_in": fox_w_in, "fox_b_f": fox_b_f, "fox_q_g": fox_q_g, "fox_k_g": fox_k_g,
            "fox_w_out": fox_w_out, "mix_norm_g": mix_norm_g, "mlp_norm_g": mlp_norm_g,
            "mlp_w1": mlp_w1, "mlp_w2": mlp_w2}


def reference(x, gmlp_w_in, gmlp_ln_g, gmlp_ln_b, gmlp_w_s, gmlp_b_s, gmlp_w_out,
              fox_w_in, fox_b_f, fox_q_g, fox_k_g, fox_w_out,
              mix_norm_g, mlp_norm_g, mlp_w1, mlp_w2):
    for i in range(DEPTH):
        h = rms_norm(x, mix_norm_g[i])
        j = i // N_MIXERS
        if i % N_MIXERS == 0:
            x = x + gmlp_mixer(h, gmlp_w_in[j], gmlp_ln_g[j], gmlp_ln_b[j],
                               gmlp_w_s[j], gmlp_b_s[j], gmlp_w_out[j])
        else:
            x = x + fox_mixer(h, fox_w_in[j], fox_b_f[j], fox_q_g[j], fox_k_g[j], fox_w_out[j])
        h = rms_norm(x, mlp_norm_g[i])
        x = x + sqrelu_mlp(h, mlp_w1[i], mlp_w2[i])
    return x
```

```python
import functools
import math

import numpy as np
import jax
import jax.numpy as jnp
from jax import lax
from jax.experimental import pallas as pl
from jax.experimental.pallas import tpu as pltpu

F32 = jnp.float32
BF16 = jnp.bfloat16

CHUNK = 128
GMLP_GROUPS = 8
HEAD_DIM = 64
RMS_EPS = 1e-6
LN_EPS = 1e-5
LOG2E = math.log2(math.e)

LANES = 128
MXU_DIM = 256
VMEM_LIMIT_BYTES = 56 * 1024 * 1024

MLP_ROWS = 1024
MLP_FF_TILE = 512
GMLP_ROWS = 512
PROJ_ROWS = 512
ATT_Q = 256
ATT_K = 256
AUG_PER_HEAD = 6
NEG_BIG = -1e30


def _compiler_params(n_axes):
    return pltpu.CompilerParams(
        dimension_semantics=("arbitrary",) * n_axes,
        vmem_limit_bytes=VMEM_LIMIT_BYTES,
    )


def _resident(shape):
    nd = len(shape)
    return pl.BlockSpec(shape, lambda *_: (0,) * nd, pipeline_mode=pl.Buffered(1))


def _rms_norm(x, g):
    ms = jnp.mean(x * x, axis=-1, keepdims=True)
    return x * lax.rsqrt(ms + RMS_EPS) * g


def _gelu_tanh(x):
    c = math.sqrt(2.0 / math.pi)
    return x * (0.5 * (1.0 + jnp.tanh(c * (x + 0.044715 * (x * x * x)))))


def _dot(a, b):
    return jnp.dot(a, b, preferred_element_type=F32)


def _mlp_kernel(*refs, fused_out_proj):
    if fused_out_proj:
        x_ref, og_ref, wo_ref, g_ref, w1_ref, w2_ref, o_ref, h_ref = refs
    else:
        x_ref, g_ref, w1_ref, w2_ref, o_ref, h_ref = refs
    j = pl.program_id(1)

    @pl.when(j == 0)
    def _():
        x = x_ref[...]
        if fused_out_proj:
            x = x + _dot(og_ref[...], wo_ref[...])
        h_ref[...] = _rms_norm(x, g_ref[...]).astype(BF16)
        o_ref[...] = x

    a = _dot(h_ref[...], w1_ref[...])
    a = jnp.maximum(a, 0.0)
    a = (a * a).astype(BF16)
    o_ref[...] += _dot(a, w2_ref[...])


def _mlp_layer(x, g, w1, w2, og=None, wo=None):
    t, d = x.shape
    ff = w1.shape[1]
    tm, tf = MLP_ROWS, MLP_FF_TILE
    fused = og is not None
    in_specs = [pl.BlockSpec((tm, d), lambda i, j: (i, 0))]
    args = [x]
    if fused:
        in_specs += [pl.BlockSpec((tm, og.shape[1]), lambda i, j: (i, 0)),
                     _resident(wo.shape)]
        args += [og, wo]
    in_specs += [
        _resident((1, d)),
        pl.BlockSpec((d, tf), lambda i, j: (0, j)),
        pl.BlockSpec((tf, d), lambda i, j: (j, 0)),
    ]
    args += [g.reshape(1, d), w1, w2]
    return pl.pallas_call(
        functools.partial(_mlp_kernel, fused_out_proj=fused),
        grid=(t // tm, ff // tf),
        in_specs=in_specs,
        out_specs=pl.BlockSpec((tm, d), lambda i, j: (i, 0)),
        out_shape=jax.ShapeDtypeStruct((t, d), F32),
        scratch_shapes=[pltpu.VMEM((tm, d), BF16)],
        compiler_params=_compiler_params(2),
        name="mlp_fused" if fused else "mlp",
    )(*args)


def _gmlp_kernel(x_ref, g_ref, win_ref, lng_ref, lnb_ref, ws_ref, bst_ref, wout_ref,
                 o_ref, vn_ref, gated_ref):
    tm = x_ref.shape[0]
    e = lng_ref.shape[1]
    gd = e // GMLP_GROUPS
    x = x_ref[...]
    h = _rms_norm(x, g_ref[...]).astype(BF16)

    v = _gelu_tanh(_dot(h, win_ref[:, e:]))
    mu = jnp.mean(v, axis=-1, keepdims=True)
    vc = v - mu
    var = jnp.mean(vc * vc, axis=-1, keepdims=True)
    vn_ref[...] = (vc * lax.rsqrt(var + LN_EPS) * lng_ref[...] + lnb_ref[...]).astype(BF16)

    row = lax.broadcasted_iota(jnp.int32, (CHUNK, CHUNK), 0)
    col = lax.broadcasted_iota(jnp.int32, (CHUNK, CHUNK), 1)
    causal = row >= col
    for grp in range(GMLP_GROUPS):
        cols = slice(grp * gd, (grp + 1) * gd)
        w = jnp.where(causal, ws_ref[grp], 0.0).astype(BF16)
        bias = bst_ref[:, grp:grp + 1]
        u = _gelu_tanh(_dot(h, win_ref[:, cols]))
        for c in range(tm // CHUNK):
            rows = slice(c * CHUNK, (c + 1) * CHUNK)
            s = _dot(w, vn_ref[rows, cols]) + bias
            gated_ref[rows, cols] = (u[rows] * s).astype(BF16)

    o_ref[...] = x + _dot(gated_ref[...], wout_ref[...])


def _gmlp_layer(x, g, w_in, ln_g, ln_b, w_s, b_s, w_out):
    t, d = x.shape
    e = ln_g.shape[0]
    tm = GMLP_ROWS
    return pl.pallas_call(
        _gmlp_kernel,
        grid=(t // tm,),
        in_specs=[
            pl.BlockSpec((tm, d), lambda i: (i, 0)),
            _resident((1, d)),
            _resident(w_in.shape),
            _resident((1, e)),
            _resident((1, e)),
            _resident(w_s.shape),
            _resident((CHUNK, GMLP_GROUPS)),
            _resident(w_out.shape),
        ],
        out_specs=pl.BlockSpec((tm, d), lambda i: (i, 0)),
        out_shape=jax.ShapeDtypeStruct((t, d), F32),
        scratch_shapes=[pltpu.VMEM((tm, e), BF16), pltpu.VMEM((tm, e), BF16)],
        compiler_params=_compiler_params(1),
        name="gmlp",
    )(x, g.reshape(1, d), w_in, ln_g.reshape(1, e), ln_b.reshape(1, e), w_s, b_s.T, w_out)


def _split3(x):
    hi = x.astype(BF16)
    r = x - hi.astype(F32)
    mid = r.astype(BF16)
    lo = (r - mid.astype(F32)).astype(BF16)
    return hi, mid, lo


def _fox_proj_kernel(x_ref, g_ref, w_ref, wf_ref, bf_ref, qg_ref, kg_ref, gmat_ref,
                     eq_ref, ek_ref, oneq_ref, onek_ref,
                     q_ref, k_ref, v_ref, gate_ref, qa_ref, ka_ref, carry_ref,
                     *, blocks_per_seq):
    tm = x_ref.shape[0]
    wdt = q_ref.shape[1]
    i = pl.program_id(0)
    h = _rms_norm(x_ref[...], g_ref[...]).astype(BF16)

    gmat = gmat_ref[...]
    for base, gain_ref, dst in ((0, qg_ref, q_ref), (wdt, kg_ref, k_ref)):
        for sl in range(wdt // MXU_DIM):
            cols = slice(sl * MXU_DIM, (sl + 1) * MXU_DIM)
            y = _dot(h, w_ref[:, base + sl * MXU_DIM: base + (sl + 1) * MXU_DIM])
            ms = _dot((y * y).astype(BF16), gmat)
            dst[:, cols] = (y * lax.rsqrt(ms + RMS_EPS) * gain_ref[:, cols]).astype(BF16)
    for sl in range(wdt // MXU_DIM):
        cols = slice(sl * MXU_DIM, (sl + 1) * MXU_DIM)
        v_ref[:, cols] = _dot(h, w_ref[:, 2 * wdt + sl * MXU_DIM: 2 * wdt + (sl + 1) * MXU_DIM]).astype(BF16)
        gt = _dot(h, w_ref[:, 3 * wdt + sl * MXU_DIM: 3 * wdt + (sl + 1) * MXU_DIM])
        gate_ref[:, cols] = (1.0 / (1.0 + jnp.exp(-gt))).astype(BF16)

    f = _dot(h, wf_ref[...]) + bf_ref[...]
    logf = jnp.minimum(f, 0.0) - jnp.log(1.0 + jnp.exp(-jnp.abs(f)))
    row = lax.broadcasted_iota(jnp.int32, (tm, tm), 0)
    col = lax.broadcasted_iota(jnp.int32, (tm, tm), 1)
    tril = jnp.where(row >= col, 1.0, 0.0).astype(BF16)
    hi, mid, lo = _split3(logf)
    first = (i % blocks_per_seq) == 0
    carry = jnp.where(first, 0.0, carry_ref[0:1, :])
    c = _dot(tril, hi) + _dot(tril, mid) + _dot(tril, lo) + carry
    carry_ref[0:1, :] = c[tm - 1:tm, :]

    pieces = jnp.concatenate(_split3(c * LOG2E), axis=1)
    qa_ref[...] = (_dot(pieces, eq_ref[...]) + oneq_ref[...]).astype(BF16)
    ka_ref[...] = (onek_ref[...] - _dot(pieces, ek_ref[...])).astype(BF16)


def _bias_slab_constants(n_heads):
    eq = np.zeros((3 * LANES, LANES), np.float32)
    ek = np.zeros((3 * LANES, LANES), np.float32)
    oneq = np.zeros((1, LANES), np.float32)
    onek = np.zeros((1, LANES), np.float32)
    for h in range(n_heads):
        for p in range(3):
            eq[p * LANES + h, AUG_PER_HEAD * h + p] = 1.0
            ek[p * LANES + h, AUG_PER_HEAD * h + 3 + p] = 1.0
            oneq[0, AUG_PER_HEAD * h + 3 + p] = 1.0
            onek[0, AUG_PER_HEAD * h + p] = 1.0
    return (jnp.asarray(eq, BF16), jnp.asarray(ek, BF16), jnp.asarray(oneq), jnp.asarray(onek))


def _head_mean_matrix():
    idx = np.arange(MXU_DIM) // HEAD_DIM
    return jnp.asarray((idx[:, None] == idx[None, :]).astype(np.float32) / HEAD_DIM, BF16)


def _fox_proj(x, g, w_main, w_f, b_f, q_gain, k_gain, seq):
    t, d = x.shape
    wdt = w_main.shape[1] // 4
    n_heads = wdt // HEAD_DIM
    tm = PROJ_ROWS
    eq, ek, oneq, onek = _bias_slab_constants(n_heads)
    gmat = _head_mean_matrix()
    row_spec = lambda width: pl.BlockSpec((tm, width), lambda i: (i, 0))
    bf = lambda width: jax.ShapeDtypeStruct((t, width), BF16)
    return pl.pallas_call(
        functools.partial(_fox_proj_kernel, blocks_per_seq=seq // tm),
        grid=(t // tm,),
        in_specs=[
            row_spec(d), _resident((1, d)), _resident(w_main.shape), _resident(w_f.shape),
            _resident((1, LANES)), _resident((1, wdt)), _resident((1, wdt)),
            _resident(gmat.shape), _resident(eq.shape), _resident(ek.shape),
            _resident((1, LANES)), _resident((1, LANES)),
        ],
        out_specs=[row_spec(wdt), row_spec(wdt), row_spec(wdt), row_spec(wdt),
                   row_spec(LANES), row_spec(LANES)],
        out_shape=[bf(wdt), bf(wdt), bf(wdt), bf(wdt), bf(LANES), bf(LANES)],
        scratch_shapes=[pltpu.VMEM((8, LANES), F32)],
        compiler_params=_compiler_params(1),
        name="fox_proj",
    )(x, g.reshape(1, d), w_main, w_f, b_f, q_gain, k_gain, gmat, eq, ek, oneq, onek)


def _fox_attn_kernel(q_ref, qa_ref, k_ref, ka_ref, v_ref, gate_ref, o_ref,
                     kcat_ref, vext_ref):
    tq = q_ref.shape[0]
    tk = ATT_K
    seq = k_ref.shape[0]
    p = pl.program_id(1)
    i = pl.program_id(2)
    lane = lax.broadcasted_iota(jnp.int32, (1, LANES), 1)
    ones_lane = (HEAD_DIM, 0)

    @pl.when(i == 0)
    def _():
        k = k_ref[...]
        ka = ka_ref[...]
        v = v_ref[...]
        for e in range(2):
            head = 2 * p + e
            in_half = (lane >= HEAD_DIM * e) & (lane < HEAD_DIM * (e + 1))
            half = jnp.where(in_half, 1.0, 0.0).astype(BF16)
            aug_lo = AUG_PER_HEAD * head
            in_aug = (lane >= aug_lo) & (lane < aug_lo + AUG_PER_HEAD)
            augm = jnp.where(in_aug, 1.0, 0.0).astype(BF16)
            one = jnp.where(lane == ones_lane[e], 1.0, 0.0).astype(BF16)
            kcat_ref[e, :, :LANES] = k * half
            kcat_ref[e, :, LANES:] = ka * augm
            vext_ref[e] = v * half + one

    qcat = jnp.concatenate([q_ref[...], qa_ref[...]], axis=1)

    def scores(e, ks):
        kc = kcat_ref[e, pl.ds(ks, tk), :]
        return lax.dot_general(qcat, kc, (((1,), (1,)), ((), ())),
                               preferred_element_type=F32)

    def update(e, ks, s, m, acc):
        m_new = jnp.maximum(m, jnp.max(s, axis=-1, keepdims=True))
        alpha = jnp.exp2(m - m_new)
        prob = jnp.exp2(s - m_new).astype(BF16)
        acc = acc * alpha + _dot(prob, vext_ref[e, pl.ds(ks, tk), :])
        return m_new, acc

    def body(kb, carry):
        ks = pl.multiple_of(kb * tk, tk)
        out = []
        for e in range(2):
            m, acc = carry[e]
            out.append(update(e, ks, scores(e, ks), m, acc))
        return tuple(out)

    init = tuple((jnp.full((tq, 1), NEG_BIG, F32), jnp.zeros((tq, LANES), F32)) for _ in range(2))
    carry = lax.fori_loop(0, i, body, init)

    ks = pl.multiple_of(i * tk, tk)
    t_idx = lax.broadcasted_iota(jnp.int32, (tq, tk), 0)
    s_idx = lax.broadcasted_iota(jnp.int32, (tq, tk), 1)
    visible = t_idx >= s_idx
    res = []
    for e in range(2):
        m, acc = carry[e]
        s = jnp.where(visible, scores(e, ks), NEG_BIG)
        m, acc = update(e, ks, s, m, acc)
        denom = acc[:, ones_lane[e]:ones_lane[e] + 1]
        res.append(acc / denom)
    o = jnp.where(lane < HEAD_DIM, res[0], res[1])
    o_ref[...] = (o * gate_ref[...].astype(F32)).astype(BF16)


def _fox_attn(q, k, v, gate, qa, ka, batch, seq):
    t, wdt = q.shape
    tq = ATT_Q
    assert ATT_Q == ATT_K
    nq = seq // tq
    pairs = wdt // LANES
    qspec = lambda col: pl.BlockSpec((tq, LANES), (lambda b, p, i: (b * nq + i, p)) if col else
                                     (lambda b, p, i: (b * nq + i, 0)))
    kspec = lambda col: pl.BlockSpec((seq, LANES), (lambda b, p, i: (b, p)) if col else
                                     (lambda b, p, i: (b, 0)))
    return pl.pallas_call(
        _fox_attn_kernel,
        grid=(batch, pairs, nq),
        in_specs=[qspec(True), qspec(False), kspec(True), kspec(False), kspec(True), qspec(True)],
        out_specs=qspec(True),
        out_shape=jax.ShapeDtypeStruct((t, wdt), BF16),
        scratch_shapes=[pltpu.VMEM((2, seq, 2 * LANES), BF16), pltpu.VMEM((2, seq, LANES), BF16)],
        compiler_params=_compiler_params(3),
        name="fox_attn",
    )(q, qa, k, ka, v, gate)


def kernel(x, gmlp_w_in, gmlp_ln_g, gmlp_ln_b, gmlp_w_s, gmlp_b_s, gmlp_w_out, fox_w_in, fox_b_f, fox_q_g, fox_k_g, fox_w_out, mix_norm_g, mlp_norm_g, mlp_w1, mlp_w2):
    batch, seq, d = x.shape
    depth = mix_norm_g.shape[0]
    wdt = fox_w_out.shape[1]
    n_heads = wdt // HEAD_DIM
    xs = x.reshape(batch * seq, d)

    for layer in range(depth):
        j = layer // 2
        og = wo = None
        if layer % 2 == 0:
            xs = _gmlp_layer(xs, mix_norm_g[layer], gmlp_w_in[j].astype(BF16), gmlp_ln_g[j],
                             gmlp_ln_b[j], gmlp_w_s[j], gmlp_b_s[j], gmlp_w_out[j].astype(BF16))
        else:
            w_in = fox_w_in[j]
            w_f = jnp.pad(w_in[:, 4 * wdt:], ((0, 0), (0, LANES - n_heads))).astype(BF16)
            b_f = jnp.pad(fox_b_f[j], (0, LANES - n_heads)).reshape(1, LANES)
            q_gain = (jnp.tile(fox_q_g[j], n_heads) * (HEAD_DIM ** -0.5 * LOG2E)).reshape(1, wdt)
            k_gain = jnp.tile(fox_k_g[j], n_heads).reshape(1, wdt)
            q, k, v, gate, qa, ka = _fox_proj(xs, mix_norm_g[layer], w_in[:, :4 * wdt].astype(BF16),
                                              w_f, b_f, q_gain, k_gain, seq)
            og = _fox_attn(q, k, v, gate, qa, ka, batch, seq)
            wo = fox_w_out[j].astype(BF16)
        xs = _mlp_layer(xs, mlp_norm_g[layer], mlp_w1[layer].astype(BF16), mlp_w2[layer].astype(BF16),
                        og=og, wo=wo)
    return xs.reshape(batch, seq, d)
```

```python
import functools
import math

import numpy as np
import jax
import jax.numpy as jnp
from jax import lax
from jax.experimental import pallas as pl
from jax.experimental.pallas import tpu as pltpu

F32 = jnp.float32
BF16 = jnp.bfloat16

CHUNK = 128
GMLP_GROUPS = 8
HEAD_DIM = 64
RMS_EPS = 1e-6
LN_EPS = 1e-5
LOG2E = math.log2(math.e)

LANES = 128
MXU_DIM = 256
VMEM_LIMIT_BYTES = 56 * 1024 * 1024

MLP_ROWS = 1024
MLP_FF_TILE = 512
GMLP_ROWS = 512
PROJ_ROWS = 512
ATT_Q = 256
ATT_K = 256
AUG_PER_HEAD = 6
NEG_BIG = -1e30


def _compiler_params(n_axes):
    return pltpu.CompilerParams(
        dimension_semantics=("arbitrary",) * n_axes,
        vmem_limit_bytes=VMEM_LIMIT_BYTES,
    )


def _resident(shape):
    nd = len(shape)
    return pl.BlockSpec(shape, lambda *_: (0,) * nd, pipeline_mode=pl.Buffered(1))


def _rms_norm(x, g):
    ms = jnp.mean(x * x, axis=-1, keepdims=True)
    return x * lax.rsqrt(ms + RMS_EPS) * g


def _gelu_tanh(x):
    c = math.sqrt(2.0 / math.pi)
    return x * (0.5 * (1.0 + jnp.tanh(c * (x + 0.044715 * (x * x * x)))))


def _dot(a, b):
    return jnp.dot(a, b, preferred_element_type=F32)


def _mlp_kernel(*refs, fused_out_proj):
    if fused_out_proj:
        x_ref, og_ref, wo_ref, g_ref, w1_ref, w2_ref, o_ref, h_ref = refs
    else:
        x_ref, g_ref, w1_ref, w2_ref, o_ref, h_ref = refs
    j = pl.program_id(1)

    @pl.when(j == 0)
    def _():
        x = x_ref[...]
        if fused_out_proj:
            x = x + _dot(og_ref[...], wo_ref[...])
        h_ref[...] = _rms_norm(x, g_ref[...]).astype(BF16)
        o_ref[...] = x

    a = _dot(h_ref[...], w1_ref[...])
    a = jnp.maximum(a, 0.0)
    a = (a * a).astype(BF16)
    o_ref[...] += _dot(a, w2_ref[...])


def _mlp_layer(x, g, w1, w2, og=None, wo=None):
    t, d = x.shape
    ff = w1.shape[1]
    tm, tf = MLP_ROWS, MLP_FF_TILE
    fused = og is not None
    in_specs = [pl.BlockSpec((tm, d), lambda i, j: (i, 0))]
    args = [x]
    if fused:
        in_specs += [pl.BlockSpec((tm, og.shape[1]), lambda i, j: (i, 0)),
                     _resident(wo.shape)]
        args += [og, wo]
    in_specs += [
        _resident((1, d)),
        pl.BlockSpec((d, tf), lambda i, j: (0, j)),
        pl.BlockSpec((tf, d), lambda i, j: (j, 0)),
    ]
    args += [g.reshape(1, d), w1, w2]
    return pl.pallas_call(
        functools.partial(_mlp_kernel, fused_out_proj=fused),
        grid=(t // tm, ff // tf),
        in_specs=in_specs,
        out_specs=pl.BlockSpec((tm, d), lambda i, j: (i, 0)),
        out_shape=jax.ShapeDtypeStruct((t, d), F32),
        scratch_shapes=[pltpu.VMEM((tm, d), BF16)],
        compiler_params=_compiler_params(2),
        name="mlp_fused" if fused else "mlp",
    )(*args)


def _gmlp_kernel(x_ref, g_ref, win_ref, lng_ref, lnb_ref, ws_ref, bst_ref, wout_ref,
                 o_ref, vn_ref, gated_ref):
    tm = x_ref.shape[0]
    e = lng_ref.shape[1]
    gd = e // GMLP_GROUPS
    x = x_ref[...]
    h = _rms_norm(x, g_ref[...]).astype(BF16)

    v = _gelu_tanh(_dot(h, win_ref[:, e:]))
    mu = jnp.mean(v, axis=-1, keepdims=True)
    vc = v - mu
    var = jnp.mean(vc * vc, axis=-1, keepdims=True)
    vn_ref[...] = (vc * lax.rsqrt(var + LN_EPS) * lng_ref[...] + lnb_ref[...]).astype(BF16)

    row = lax.broadcasted_iota(jnp.int32, (CHUNK, CHUNK), 0)
    col = lax.broadcasted_iota(jnp.int32, (CHUNK, CHUNK), 1)
    causal = row >= col
    for grp in range(GMLP_GROUPS):
        cols = slice(grp * gd, (grp + 1) * gd)
        w = jnp.where(causal, ws_ref[grp], 0.0).astype(BF16)
        bias = bst_ref[:, grp:grp + 1]
        u = _gelu_tanh(_dot(h, win_ref[:, cols]))
        for c in range(tm // CHUNK):
            rows = slice(c * CHUNK, (c + 1) * CHUNK)
            s = _dot(w, vn_ref[rows, cols]) + bias
            gated_ref[rows, cols] = (u[rows] * s).astype(BF16)

    o_ref[...] = x + _dot(gated_ref[...], wout_ref[...])


def _gmlp_layer(x, g, w_in, ln_g, ln_b, w_s, b_s, w_out):
    t, d = x.shape
    e = ln_g.shape[0]
    tm = GMLP_ROWS
    return pl.pallas_call(
        _gmlp_kernel,
        grid=(t // tm,),
        in_specs=[
            pl.BlockSpec((tm, d), lambda i: (i, 0)),
            _resident((1, d)),
            _resident(w_in.shape),
            _resident((1, e)),
            _resident((1, e)),
            _resident(w_s.shape),
            _resident((CHUNK, GMLP_GROUPS)),
            _resident(w_out.shape),
        ],
        out_specs=pl.BlockSpec((tm, d), lambda i: (i, 0)),
        out_shape=jax.ShapeDtypeStruct((t, d), F32),
        scratch_shapes=[pltpu.VMEM((tm, e), BF16), pltpu.VMEM((tm, e), BF16)],
        compiler_params=_compiler_params(1),
        name="gmlp",
    )(x, g.reshape(1, d), w_in, ln_g.reshape(1, e), ln_b.reshape(1, e), w_s, b_s.T, w_out)


def _split3(x):
    hi = x.astype(BF16)
    r = x - hi.astype(F32)
    mid = r.astype(BF16)
    lo = (r - mid.astype(F32)).astype(BF16)
    return hi, mid, lo


def _fox_proj_kernel(x_ref, g_ref, w_ref, wf_ref, bf_ref, qg_ref, kg_ref, gmat_ref,
                     eq_ref, ek_ref, oneq_ref, onek_ref,
                     q_ref, k_ref, v_ref, gate_ref, qa_ref, ka_ref, carry_ref,
                     *, blocks_per_seq):
    tm = x_ref.shape[0]
    wdt = q_ref.shape[1]
    i = pl.program_id(0)
    h = _rms_norm(x_ref[...], g_ref[...]).astype(BF16)

    gmat = gmat_ref[...]
    for base, gain_ref, dst in ((0, qg_ref, q_ref), (wdt, kg_ref, k_ref)):
        for sl in range(wdt // MXU_DIM):
            cols = slice(sl * MXU_DIM, (sl + 1) * MXU_DIM)
            y = _dot(h, w_ref[:, base + sl * MXU_DIM: base + (sl + 1) * MXU_DIM])
            ms = _dot((y * y).astype(BF16), gmat)
            dst[:, cols] = (y * lax.rsqrt(ms + RMS_EPS) * gain_ref[:, cols]).astype(BF16)
    for sl in range(wdt // MXU_DIM):
        cols = slice(sl * MXU_DIM, (sl + 1) * MXU_DIM)
        v_ref[:, cols] = _dot(h, w_ref[:, 2 * wdt + sl * MXU_DIM: 2 * wdt + (sl + 1) * MXU_DIM]).astype(BF16)
        gt = _dot(h, w_ref[:, 3 * wdt + sl * MXU_DIM: 3 * wdt + (sl + 1) * MXU_DIM])
        gate_ref[:, cols] = (1.0 / (1.0 + jnp.exp(-gt))).astype(BF16)

    f = _dot(h, wf_ref[...]) + bf_ref[...]
    logf = jnp.minimum(f, 0.0) - jnp.log(1.0 + jnp.exp(-jnp.abs(f)))
    row = lax.broadcasted_iota(jnp.int32, (tm, tm), 0)
    col = lax.broadcasted_iota(jnp.int32, (tm, tm), 1)
    tril = jnp.where(row >= col, 1.0, 0.0).astype(BF16)
    hi, mid, lo = _split3(logf)
    first = (i % blocks_per_seq) == 0
    carry = jnp.where(first, 0.0, carry_ref[0:1, :])
    c = _dot(tril, hi) + _dot(tril, mid) + _dot(tril, lo) + carry
    carry_ref[0:1, :] = c[tm - 1:tm, :]

    pieces = jnp.concatenate(_split3(c * LOG2E), axis=1)
    qa_ref[...] = (_dot(pieces, eq_ref[...]) + oneq_ref[...]).astype(BF16)
    ka_ref[...] = (onek_ref[...] - _dot(pieces, ek_ref[...])).astype(BF16)


def _bias_slab_constants(n_heads):
    eq = np.zeros((3 * LANES, LANES), np.float32)
    ek = np.zeros((3 * LANES, LANES), np.float32)
    oneq = np.zeros((1, LANES), np.float32)
    onek = np.zeros((1, LANES), np.float32)
    for h in range(n_heads):
        for p in range(3):
            eq[p * LANES + h, AUG_PER_HEAD * h + p] = 1.0
            ek[p * LANES + h, AUG_PER_HEAD * h + 3 + p] = 1.0
            oneq[0, AUG_PER_HEAD * h + 3 + p] = 1.0
            onek[0, AUG_PER_HEAD * h + p] = 1.0
    return (jnp.asarray(eq, BF16), jnp.asarray(ek, BF16), jnp.asarray(oneq), jnp.asarray(onek))


def _head_mean_matrix():
    idx = np.arange(MXU_DIM) // HEAD_DIM
    return jnp.asarray((idx[:, None] == idx[None, :]).astype(np.float32) / HEAD_DIM, BF16)


def _fox_proj(x, g, w_main, w_f, b_f, q_gain, k_gain, seq):
    t, d = x.shape
    wdt = w_main.shape[1] // 4
    n_heads = wdt // HEAD_DIM
    tm = PROJ_ROWS
    eq, ek, oneq, onek = _bias_slab_constants(n_heads)
    gmat = _head_mean_matrix()
    row_spec = lambda width: pl.BlockSpec((tm, width), lambda i: (i, 0))
    bf = lambda width: jax.ShapeDtypeStruct((t, width), BF16)
    return pl.pallas_call(
        functools.partial(_fox_proj_kernel, blocks_per_seq=seq // tm),
        grid=(t // tm,),
        in_specs=[
            row_spec(d), _resident((1, d)), _resident(w_main.shape), _resident(w_f.shape),
            _resident((1, LANES)), _resident((1, wdt)), _resident((1, wdt)),
            _resident(gmat.shape), _resident(eq.shape), _resident(ek.shape),
            _resident((1, LANES)), _resident((1, LANES)),
        ],
        out_specs=[row_spec(wdt), row_spec(wdt), row_spec(wdt), row_spec(wdt),
                   row_spec(LANES), row_spec(LANES)],
        out_shape=[bf(wdt), bf(wdt), bf(wdt), bf(wdt), bf(LANES), bf(LANES)],
        scratch_shapes=[pltpu.VMEM((8, LANES), F32)],
        compiler_params=_compiler_params(1),
        name="fox_proj",
    )(x, g.reshape(1, d), w_main, w_f, b_f, q_gain, k_gain, gmat, eq, ek, oneq, onek)


def _nt_dot(a, b):
    return lax.dot_general(a, b, (((1,), (1,)), ((), ())), preferred_element_type=F32)


def _fox_attn_kernel(q_ref, qa_ref, k_ref, ka_ref, v_ref, gate_ref, o_ref, kcat_ref, vt_ref):
    seq = k_ref.shape[0]
    tq = ATT_Q
    p = pl.program_id(1)
    lane = lax.broadcasted_iota(jnp.int32, (1, LANES), 1)
    sublane = lax.broadcasted_iota(jnp.int32, (LANES, 1), 0)
    ones_row = (HEAD_DIM, 0)

    k = k_ref[...]
    ka = ka_ref[...]
    v = v_ref[...].astype(F32)
    for e in range(2):
        head = 2 * p + e
        in_half = (lane >= HEAD_DIM * e) & (lane < HEAD_DIM * (e + 1))
        aug_lo = AUG_PER_HEAD * head
        in_aug = (lane >= aug_lo) & (lane < aug_lo + AUG_PER_HEAD)
        kcat_ref[e, :, :LANES] = k * jnp.where(in_half, 1.0, 0.0).astype(BF16)
        kcat_ref[e, :, LANES:] = ka * jnp.where(in_aug, 1.0, 0.0).astype(BF16)
        vext = jnp.where(in_half, v, 0.0) + jnp.where(lane == ones_row[e], 1.0, 0.0)
        vt_ref[e] = vext.T.astype(BF16)

    key_idx = lax.broadcasted_iota(jnp.int32, (tq, tq), 0)
    qry_idx = lax.broadcasted_iota(jnp.int32, (tq, tq), 1)
    visible = key_idx <= qry_idx
    for i in range(seq // tq):
        r0, r1 = i * tq, (i + 1) * tq
        qcat = jnp.concatenate([q_ref[r0:r1, :], qa_ref[r0:r1, :]], axis=1)
        res = []
        for e in range(2):
            s_diag = jnp.where(visible, _nt_dot(kcat_ref[e, r0:r1, :], qcat), NEG_BIG)
            m = jnp.max(s_diag, axis=0, keepdims=True)
            if i > 0:
                s_past = _nt_dot(kcat_ref[e, :r0, :], qcat)
                m = jnp.maximum(m, jnp.max(s_past, axis=0, keepdims=True))
                acc = _dot(vt_ref[e, :, :r0], jnp.exp2(s_past - m).astype(BF16))
                acc = acc + _dot(vt_ref[e, :, r0:r1], jnp.exp2(s_diag - m).astype(BF16))
            else:
                acc = _dot(vt_ref[e, :, r0:r1], jnp.exp2(s_diag - m).astype(BF16))
            res.append(acc / acc[ones_row[e]:ones_row[e] + 1, :])
        o = jnp.where(sublane < HEAD_DIM, res[0], res[1]).T
        o_ref[r0:r1, :] = (o * gate_ref[r0:r1, :].astype(F32)).astype(BF16)


def _fox_attn(q, k, v, gate, qa, ka, batch, seq):
    t, wdt = q.shape
    pairs = wdt // LANES
    pair_spec = pl.BlockSpec((seq, LANES), lambda b, p: (b, p))
    aug_spec = pl.BlockSpec((seq, LANES), lambda b, p: (b, 0))
    return pl.pallas_call(
        _fox_attn_kernel,
        grid=(batch, pairs),
        in_specs=[pair_spec, aug_spec, pair_spec, aug_spec, pair_spec, pair_spec],
        out_specs=pair_spec,
        out_shape=jax.ShapeDtypeStruct((t, wdt), BF16),
        scratch_shapes=[pltpu.VMEM((2, seq, 2 * LANES), BF16), pltpu.VMEM((2, LANES, seq), BF16)],
        compiler_params=_compiler_params(2),
        name="fox_attn",
    )(q, qa, k, ka, v, gate)


def kernel(x, gmlp_w_in, gmlp_ln_g, gmlp_ln_b, gmlp_w_s, gmlp_b_s, gmlp_w_out, fox_w_in, fox_b_f, fox_q_g, fox_k_g, fox_w_out, mix_norm_g, mlp_norm_g, mlp_w1, mlp_w2):
    batch, seq, d = x.shape
    depth = mix_norm_g.shape[0]
    wdt = fox_w_out.shape[1]
    n_heads = wdt // HEAD_DIM
    xs = x.reshape(batch * seq, d)

    for layer in range(depth):
        j = layer // 2
        og = wo = None
        if layer % 2 == 0:
            xs = _gmlp_layer(xs, mix_norm_g[layer], gmlp_w_in[j].astype(BF16), gmlp_ln_g[j],
                             gmlp_ln_b[j], gmlp_w_s[j], gmlp_b_s[j], gmlp_w_out[j].astype(BF16))
        else:
            w_in = fox_w_in[j]
            w_f = jnp.pad(w_in[:, 4 * wdt:], ((0, 0), (0, LANES - n_heads))).astype(BF16)
            b_f = jnp.pad(fox_b_f[j], (0, LANES - n_heads)).reshape(1, LANES)
            q_gain = (jnp.tile(fox_q_g[j], n_heads) * (HEAD_DIM ** -0.5 * LOG2E)).reshape(1, wdt)
            k_gain = jnp.tile(fox_k_g[j], n_heads).reshape(1, wdt)
            q, k, v, gate, qa, ka = _fox_proj(xs, mix_norm_g[layer], w_in[:, :4 * wdt].astype(BF16),
                                              w_f, b_f, q_gain, k_gain, seq)
            og = _fox_attn(q, k, v, gate, qa, ka, batch, seq)
            wo = fox_w_out[j].astype(BF16)
        xs = _mlp_layer(xs, mlp_norm_g[layer], mlp_w1[layer].astype(BF16), mlp_w2[layer].astype(BF16),
                        og=og, wo=wo)
    return xs.reshape(batch, seq, d)
```

```python
import functools
import math

import numpy as np
import jax
import jax.numpy as jnp
from jax import lax
from jax.experimental import pallas as pl
from jax.experimental.pallas import tpu as pltpu

F32 = jnp.float32
BF16 = jnp.bfloat16

CHUNK = 128
GMLP_GROUPS = 8
HEAD_DIM = 64
RMS_EPS = 1e-6
LN_EPS = 1e-5
LOG2E = math.log2(math.e)

LANES = 128
MXU_DIM = 256
VMEM_LIMIT_BYTES = 56 * 1024 * 1024

MLP_ROWS = 1024
MLP_FF_TILE = 512
GMLP_ROWS = 512
PROJ_ROWS = 512
ATT_Q = 256
ONES_ROWS = 16
AUG_PER_HEAD = 6
NEG_BIG = -1e30


def _compiler_params(n_axes):
    return pltpu.CompilerParams(
        dimension_semantics=("arbitrary",) * n_axes,
        vmem_limit_bytes=VMEM_LIMIT_BYTES,
    )


def _resident(shape):
    nd = len(shape)
    return pl.BlockSpec(shape, lambda *_: (0,) * nd, pipeline_mode=pl.Buffered(1))


def _rms_norm(x, g):
    ms = jnp.mean(x * x, axis=-1, keepdims=True)
    return x * lax.rsqrt(ms + RMS_EPS) * g


def _gelu_tanh(x):
    c = math.sqrt(2.0 / math.pi)
    return x * (0.5 * (1.0 + jnp.tanh(c * (x + 0.044715 * (x * x * x)))))


def _dot(a, b):
    return jnp.dot(a, b, preferred_element_type=F32)


def _mlp_kernel(*refs, fused_out_proj):
    if fused_out_proj:
        x_ref, og_ref, wo_ref, g_ref, w1_ref, w2_ref, o_ref, h_ref = refs
    else:
        x_ref, g_ref, w1_ref, w2_ref, o_ref, h_ref = refs
    j = pl.program_id(1)

    @pl.when(j == 0)
    def _():
        x = x_ref[...]
        if fused_out_proj:
            x = x + _dot(og_ref[...], wo_ref[...])
        h_ref[...] = _rms_norm(x, g_ref[...]).astype(BF16)
        o_ref[...] = x

    a = _dot(h_ref[...], w1_ref[...])
    a = jnp.maximum(a, 0.0)
    a = (a * a).astype(BF16)
    o_ref[...] += _dot(a, w2_ref[...])


def _mlp_layer(x, g, w1, w2, og=None, wo=None):
    t, d = x.shape
    ff = w1.shape[1]
    tm, tf = MLP_ROWS, MLP_FF_TILE
    fused = og is not None
    in_specs = [pl.BlockSpec((tm, d), lambda i, j: (i, 0))]
    args = [x]
    if fused:
        in_specs += [pl.BlockSpec((tm, og.shape[1]), lambda i, j: (i, 0)),
                     _resident(wo.shape)]
        args += [og, wo]
    in_specs += [
        _resident((1, d)),
        pl.BlockSpec((d, tf), lambda i, j: (0, j)),
        pl.BlockSpec((tf, d), lambda i, j: (j, 0)),
    ]
    args += [g.reshape(1, d), w1, w2]
    return pl.pallas_call(
        functools.partial(_mlp_kernel, fused_out_proj=fused),
        grid=(t // tm, ff // tf),
        in_specs=in_specs,
        out_specs=pl.BlockSpec((tm, d), lambda i, j: (i, 0)),
        out_shape=jax.ShapeDtypeStruct((t, d), F32),
        scratch_shapes=[pltpu.VMEM((tm, d), BF16)],
        compiler_params=_compiler_params(2),
        name="mlp_fused" if fused else "mlp",
    )(*args)


def _gmlp_kernel(x_ref, g_ref, win_ref, lng_ref, lnb_ref, ws_ref, bst_ref, wout_ref,
                 o_ref, vn_ref, gated_ref):
    tm = x_ref.shape[0]
    e = lng_ref.shape[1]
    gd = e // GMLP_GROUPS
    x = x_ref[...]
    h = _rms_norm(x, g_ref[...]).astype(BF16)

    v = _gelu_tanh(_dot(h, win_ref[:, e:]))
    mu = jnp.mean(v, axis=-1, keepdims=True)
    vc = v - mu
    var = jnp.mean(vc * vc, axis=-1, keepdims=True)
    vn_ref[...] = (vc * lax.rsqrt(var + LN_EPS) * lng_ref[...] + lnb_ref[...]).astype(BF16)

    row = lax.broadcasted_iota(jnp.int32, (CHUNK, CHUNK), 0)
    col = lax.broadcasted_iota(jnp.int32, (CHUNK, CHUNK), 1)
    causal = row >= col
    for grp in range(GMLP_GROUPS):
        cols = slice(grp * gd, (grp + 1) * gd)
        w = jnp.where(causal, ws_ref[grp], 0.0).astype(BF16)
        bias = bst_ref[:, grp:grp + 1]
        u = _gelu_tanh(_dot(h, win_ref[:, cols]))
        for c in range(tm // CHUNK):
            rows = slice(c * CHUNK, (c + 1) * CHUNK)
            s = _dot(w, vn_ref[rows, cols]) + bias
            gated_ref[rows, cols] = (u[rows] * s).astype(BF16)

    o_ref[...] = x + _dot(gated_ref[...], wout_ref[...])


def _gmlp_layer(x, g, w_in, ln_g, ln_b, w_s, b_s, w_out):
    t, d = x.shape
    e = ln_g.shape[0]
    tm = GMLP_ROWS
    return pl.pallas_call(
        _gmlp_kernel,
        grid=(t // tm,),
        in_specs=[
            pl.BlockSpec((tm, d), lambda i: (i, 0)),
            _resident((1, d)),
            _resident(w_in.shape),
            _resident((1, e)),
            _resident((1, e)),
            _resident(w_s.shape),
            _resident((CHUNK, GMLP_GROUPS)),
            _resident(w_out.shape),
        ],
        out_specs=pl.BlockSpec((tm, d), lambda i: (i, 0)),
        out_shape=jax.ShapeDtypeStruct((t, d), F32),
        scratch_shapes=[pltpu.VMEM((tm, e), BF16), pltpu.VMEM((tm, e), BF16)],
        compiler_params=_compiler_params(1),
        name="gmlp",
    )(x, g.reshape(1, d), w_in, ln_g.reshape(1, e), ln_b.reshape(1, e), w_s, b_s.T, w_out)


def _split3(x):
    hi = x.astype(BF16)
    r = x - hi.astype(F32)
    mid = r.astype(BF16)
    lo = (r - mid.astype(F32)).astype(BF16)
    return hi, mid, lo


def _fox_proj_kernel(x_ref, g_ref, w_ref, wf_ref, bf_ref, qg_ref, kg_ref, gmat_ref,
                     eq_ref, ek_ref, oneq_ref, onek_ref,
                     q_ref, k_ref, v_ref, gate_ref, qa_ref, ka_ref, carry_ref,
                     *, blocks_per_seq):
    tm = x_ref.shape[0]
    wdt = q_ref.shape[1]
    i = pl.program_id(0)
    h = _rms_norm(x_ref[...], g_ref[...]).astype(BF16)

    gmat = gmat_ref[...]
    for base, gain_ref, dst in ((0, qg_ref, q_ref), (wdt, kg_ref, k_ref)):
        for sl in range(wdt // MXU_DIM):
            cols = slice(sl * MXU_DIM, (sl + 1) * MXU_DIM)
            y = _dot(h, w_ref[:, base + sl * MXU_DIM: base + (sl + 1) * MXU_DIM])
            ms = _dot((y * y).astype(BF16), gmat)
            dst[:, cols] = (y * lax.rsqrt(ms + RMS_EPS) * gain_ref[:, cols]).astype(BF16)
    for sl in range(wdt // MXU_DIM):
        cols = slice(sl * MXU_DIM, (sl + 1) * MXU_DIM)
        v_ref[:, cols] = _dot(h, w_ref[:, 2 * wdt + sl * MXU_DIM: 2 * wdt + (sl + 1) * MXU_DIM]).astype(BF16)
        gt = _dot(h, w_ref[:, 3 * wdt + sl * MXU_DIM: 3 * wdt + (sl + 1) * MXU_DIM])
        gate_ref[:, cols] = (1.0 / (1.0 + jnp.exp(-gt))).astype(BF16)

    f = _dot(h, wf_ref[...]) + bf_ref[...]
    logf = jnp.minimum(f, 0.0) - jnp.log(1.0 + jnp.exp(-jnp.abs(f)))
    row = lax.broadcasted_iota(jnp.int32, (tm, tm), 0)
    col = lax.broadcasted_iota(jnp.int32, (tm, tm), 1)
    tril = jnp.where(row >= col, 1.0, 0.0).astype(BF16)
    hi, mid, lo = _split3(logf)
    first = (i % blocks_per_seq) == 0
    carry = jnp.where(first, 0.0, carry_ref[0:1, :])
    c = _dot(tril, hi) + _dot(tril, mid) + _dot(tril, lo) + carry
    carry_ref[0:1, :] = c[tm - 1:tm, :]

    pieces = jnp.concatenate(_split3(c * LOG2E), axis=1)
    qa_ref[...] = (_dot(pieces, eq_ref[...]) + oneq_ref[...]).astype(BF16)
    ka_ref[...] = (onek_ref[...] - _dot(pieces, ek_ref[...])).astype(BF16)


def _bias_slab_constants(n_heads):
    eq = np.zeros((3 * LANES, LANES), np.float32)
    ek = np.zeros((3 * LANES, LANES), np.float32)
    oneq = np.zeros((1, LANES), np.float32)
    onek = np.zeros((1, LANES), np.float32)
    for h in range(n_heads):
        for p in range(3):
            eq[p * LANES + h, AUG_PER_HEAD * h + p] = 1.0
            ek[p * LANES + h, AUG_PER_HEAD * h + 3 + p] = 1.0
            oneq[0, AUG_PER_HEAD * h + 3 + p] = 1.0
            onek[0, AUG_PER_HEAD * h + p] = 1.0
    return (jnp.asarray(eq, BF16), jnp.asarray(ek, BF16), jnp.asarray(oneq), jnp.asarray(onek))


def _head_mean_matrix():
    idx = np.arange(MXU_DIM) // HEAD_DIM
    return jnp.asarray((idx[:, None] == idx[None, :]).astype(np.float32) / HEAD_DIM, BF16)


def _fox_proj(x, g, w_main, w_f, b_f, q_gain, k_gain, seq):
    t, d = x.shape
    wdt = w_main.shape[1] // 4
    n_heads = wdt // HEAD_DIM
    tm = PROJ_ROWS
    eq, ek, oneq, onek = _bias_slab_constants(n_heads)
    gmat = _head_mean_matrix()
    row_spec = lambda width: pl.BlockSpec((tm, width), lambda i: (i, 0))
    bf = lambda width: jax.ShapeDtypeStruct((t, width), BF16)
    return pl.pallas_call(
        functools.partial(_fox_proj_kernel, blocks_per_seq=seq // tm),
        grid=(t // tm,),
        in_specs=[
            row_spec(d), _resident((1, d)), _resident(w_main.shape), _resident(w_f.shape),
            _resident((1, LANES)), _resident((1, wdt)), _resident((1, wdt)),
            _resident(gmat.shape), _resident(eq.shape), _resident(ek.shape),
            _resident((1, LANES)), _resident((1, LANES)),
        ],
        out_specs=[row_spec(wdt), row_spec(wdt), row_spec(wdt), row_spec(wdt),
                   row_spec(LANES), row_spec(LANES)],
        out_shape=[bf(wdt), bf(wdt), bf(wdt), bf(wdt), bf(LANES), bf(LANES)],
        scratch_shapes=[pltpu.VMEM((8, LANES), F32)],
        compiler_params=_compiler_params(1),
        name="fox_proj",
    )(x, g.reshape(1, d), w_main, w_f, b_f, q_gain, k_gain, gmat, eq, ek, oneq, onek)


def _nt_dot(a, b):
    return lax.dot_general(a, b, (((1,), (1,)), ((), ())), preferred_element_type=F32)


def _fox_attn_kernel(q_ref, qa_ref, k_ref, ka_ref, v_ref, gate_ref, o_ref, kcat_ref, vt_ref):
    seq = k_ref.shape[0]
    tq = ATT_Q
    p = pl.program_id(1)
    lane = lax.broadcasted_iota(jnp.int32, (1, LANES), 1)

    kcat_ref[:, :LANES] = k_ref[...]
    kcat_ref[:, LANES:] = ka_ref[...]
    vt_ref[:LANES, :] = v_ref[...].astype(F32).T.astype(BF16)
    vt_ref[LANES:, :] = jnp.ones((ONES_ROWS, seq), BF16)

    head_masks = []
    for e in range(2):
        in_half = (lane >= HEAD_DIM * e) & (lane < HEAD_DIM * (e + 1))
        aug_lo = AUG_PER_HEAD * (2 * p + e)
        in_aug = (lane >= aug_lo) & (lane < aug_lo + AUG_PER_HEAD)
        head_masks.append((jnp.where(in_half, 1.0, 0.0).astype(BF16),
                           jnp.where(in_aug, 1.0, 0.0).astype(BF16)))

    key_idx = lax.broadcasted_iota(jnp.int32, (tq, 2 * tq), 0)
    qry_idx = lax.broadcasted_iota(jnp.int32, (tq, 2 * tq), 1)
    visible = key_idx <= qry_idx - jnp.where(qry_idx >= tq, tq, 0)
    def scores(i):
        r0, r1 = i * tq, (i + 1) * tq
        q = q_ref[r0:r1, :]
        qa = qa_ref[r0:r1, :]
        qboth = jnp.concatenate(
            [jnp.concatenate([q * half, qa * aug], axis=1) for half, aug in head_masks], axis=0)
        s_diag = jnp.where(visible, _nt_dot(kcat_ref[r0:r1, :], qboth), NEG_BIG)
        m = jnp.max(s_diag, axis=0, keepdims=True)
        s_past = None
        if i > 0:
            s_past = _nt_dot(kcat_ref[:r0, :], qboth)
            m = jnp.maximum(m, jnp.max(s_past, axis=0, keepdims=True))
        return s_past, s_diag, m

    def finish(i, s_past, s_diag, m):
        r0, r1 = i * tq, (i + 1) * tq
        acc = _dot(vt_ref[:, r0:r1], jnp.exp2(s_diag - m).astype(BF16))
        if i > 0:
            acc = acc + _dot(vt_ref[:, :r0], jnp.exp2(s_past - m).astype(BF16))
        inv = 1.0 / acc[LANES:LANES + 1, :]
        o = jnp.concatenate([acc[:HEAD_DIM, :tq] * inv[:, :tq],
                             acc[HEAD_DIM:LANES, tq:] * inv[:, tq:]], axis=0).T
        o_ref[r0:r1, :] = (o * gate_ref[r0:r1, :].astype(F32)).astype(BF16)

    order = list(range(seq // tq - 1, -1, -1))
    ahead = 2
    pending = [scores(i) for i in order[:ahead]]
    for n, i in enumerate(order):
        if n + ahead < len(order):
            pending.append(scores(order[n + ahead]))
        finish(i, *pending.pop(0))


def _fox_attn(q, k, v, gate, qa, ka, batch, seq):
    t, wdt = q.shape
    pairs = wdt // LANES
    pair_spec = pl.BlockSpec((seq, LANES), lambda b, p: (b, p))
    aug_spec = pl.BlockSpec((seq, LANES), lambda b, p: (b, 0))
    return pl.pallas_call(
        _fox_attn_kernel,
        grid=(batch, pairs),
        in_specs=[pair_spec, aug_spec, pair_spec, aug_spec, pair_spec, pair_spec],
        out_specs=pair_spec,
        out_shape=jax.ShapeDtypeStruct((t, wdt), BF16),
        scratch_shapes=[pltpu.VMEM((seq, 2 * LANES), BF16), pltpu.VMEM((LANES + ONES_ROWS, seq), BF16)],
        compiler_params=_compiler_params(2),
        name="fox_attn",
    )(q, qa, k, ka, v, gate)


def kernel(x, gmlp_w_in, gmlp_ln_g, gmlp_ln_b, gmlp_w_s, gmlp_b_s, gmlp_w_out, fox_w_in, fox_b_f, fox_q_g, fox_k_g, fox_w_out, mix_norm_g, mlp_norm_g, mlp_w1, mlp_w2):
    batch, seq, d = x.shape
    depth = mix_norm_g.shape[0]
    wdt = fox_w_out.shape[1]
    n_heads = wdt // HEAD_DIM
    xs = x.reshape(batch * seq, d)

    for layer in range(depth):
        j = layer // 2
        og = wo = None
        if layer % 2 == 0:
            xs = _gmlp_layer(xs, mix_norm_g[layer], gmlp_w_in[j].astype(BF16), gmlp_ln_g[j],
                             gmlp_ln_b[j], gmlp_w_s[j], gmlp_b_s[j], gmlp_w_out[j].astype(BF16))
        else:
            w_in = fox_w_in[j]
            w_f = jnp.pad(w_in[:, 4 * wdt:], ((0, 0), (0, LANES - n_heads))).astype(BF16)
            b_f = jnp.pad(fox_b_f[j], (0, LANES - n_heads)).reshape(1, LANES)
            q_gain = (jnp.tile(fox_q_g[j], n_heads) * (HEAD_DIM ** -0.5 * LOG2E)).reshape(1, wdt)
            k_gain = jnp.tile(fox_k_g[j], n_heads).reshape(1, wdt)
            q, k, v, gate, qa, ka = _fox_proj(xs, mix_norm_g[layer], w_in[:, :4 * wdt].astype(BF16),
                                              w_f, b_f, q_gain, k_gain, seq)
            og = _fox_attn(q, k, v, gate, qa, ka, batch, seq)
            wo = fox_w_out[j].astype(BF16)
        xs = _mlp_layer(xs, mlp_norm_g[layer], mlp_w1[layer].astype(BF16), mlp_w2[layer].astype(BF16),
                        og=og, wo=wo)
    return xs.reshape(batch, seq, d)
```

```python
import functools
import math

import numpy as np
import jax
import jax.numpy as jnp
from jax import lax
from jax.experimental import pallas as pl
from jax.experimental.pallas import tpu as pltpu

F32 = jnp.float32
BF16 = jnp.bfloat16

CHUNK = 128
GMLP_GROUPS = 8
HEAD_DIM = 64
RMS_EPS = 1e-6
LN_EPS = 1e-5
LOG2E = math.log2(math.e)

LANES = 128
MXU_DIM = 256
VMEM_LIMIT_BYTES = 56 * 1024 * 1024

MLP_ROWS = 1024
MLP_FF_TILE = 512
GMLP_ROWS = 512
PROJ_ROWS = 512
PROJ_SLAB = 512
ATT_Q = 256
ONES_ROWS = 16
AUG_PER_HEAD = 6
NEG_BIG = -1e30


def _compiler_params(n_axes):
    return pltpu.CompilerParams(
        dimension_semantics=("arbitrary",) * n_axes,
        vmem_limit_bytes=VMEM_LIMIT_BYTES,
    )


def _resident(shape):
    nd = len(shape)
    return pl.BlockSpec(shape, lambda *_: (0,) * nd, pipeline_mode=pl.Buffered(1))


def _layer_resident(stacked_shape, layer, width=None):
    tail = tuple(stacked_shape[1:-1]) + (width or stacked_shape[-1],)
    return pl.BlockSpec((None,) + tail, lambda *_: (layer,) + (0,) * len(tail),
                        pipeline_mode=pl.Buffered(1))


def _rms_norm(x, g):
    ms = jnp.mean(x * x, axis=-1, keepdims=True)
    return x * lax.rsqrt(ms + RMS_EPS) * g


def _gelu_tanh(x):
    c = math.sqrt(2.0 / math.pi)
    return x * (0.5 * (1.0 + jnp.tanh(c * (x + 0.044715 * (x * x * x)))))


def _dot(a, b):
    return jnp.dot(a, b, preferred_element_type=F32)


def _mlp_kernel(*refs, fused_out_proj):
    if fused_out_proj:
        x_ref, og_ref, wo_ref, g_ref, w1_ref, w2_ref, o_ref = refs
    else:
        x_ref, g_ref, w1_ref, w2_ref, o_ref = refs
    tf = MLP_FF_TILE
    x = x_ref[...]
    if fused_out_proj:
        x = x + _dot(og_ref[...], wo_ref[...])
    h = _rms_norm(x, g_ref[...]).astype(BF16)
    acc = x
    for f0 in range(0, w1_ref.shape[1], tf):
        a = jnp.maximum(_dot(h, w1_ref[:, f0:f0 + tf]), 0.0)
        acc = acc + _dot((a * a).astype(BF16), w2_ref[f0:f0 + tf, :])
    o_ref[...] = acc


def _mlp_layer(x, g, w1, w2, layer, og=None, wo=None, wo_layer=None):
    t, d = x.shape
    tm = MLP_ROWS
    fused = og is not None
    row_spec = lambda width: pl.BlockSpec((tm, width), lambda i: (i, 0))
    in_specs = [row_spec(d)]
    args = [x]
    if fused:
        in_specs += [row_spec(og.shape[1]), _layer_resident(wo.shape, wo_layer)]
        args += [og, wo]
    in_specs += [_resident((1, d)), _layer_resident(w1.shape, layer), _layer_resident(w2.shape, layer)]
    args += [g.reshape(1, d), w1, w2]
    return pl.pallas_call(
        functools.partial(_mlp_kernel, fused_out_proj=fused),
        grid=(t // tm,),
        in_specs=in_specs,
        out_specs=row_spec(d),
        out_shape=jax.ShapeDtypeStruct((t, d), F32),
        compiler_params=_compiler_params(1),
        name="mlp_fused" if fused else "mlp",
    )(*args)


def _gmlp_kernel(x_ref, g_ref, win_ref, lng_ref, lnb_ref, ws_ref, bst_ref, wout_ref,
                 o_ref, vn_ref, gated_ref):
    tm = x_ref.shape[0]
    e = lng_ref.shape[1]
    gd = e // GMLP_GROUPS
    x = x_ref[...]
    h = _rms_norm(x, g_ref[...]).astype(BF16)

    v = _gelu_tanh(_dot(h, win_ref[:, e:]))
    u_all = _gelu_tanh(_dot(h, win_ref[:, :e]))
    mu = jnp.mean(v, axis=-1, keepdims=True)
    vc = v - mu
    var = jnp.mean(vc * vc, axis=-1, keepdims=True)
    vn_ref[...] = (vc * lax.rsqrt(var + LN_EPS) * lng_ref[...] + lnb_ref[...]).astype(BF16)

    row = lax.broadcasted_iota(jnp.int32, (CHUNK, CHUNK), 0)
    col = lax.broadcasted_iota(jnp.int32, (CHUNK, CHUNK), 1)
    causal = row >= col
    for grp in range(GMLP_GROUPS):
        cols = slice(grp * gd, (grp + 1) * gd)
        w = jnp.where(causal, ws_ref[grp], 0.0).astype(BF16)
        bias = bst_ref[:, grp:grp + 1]
        for c in range(tm // CHUNK):
            rows = slice(c * CHUNK, (c + 1) * CHUNK)
            s = _dot(w, vn_ref[rows, cols]) + bias
            gated_ref[rows, cols] = (u_all[rows, cols] * s).astype(BF16)

    o_ref[...] = x + _dot(gated_ref[...], wout_ref[...])


def _gmlp_layer(x, g, w_in, ln_g, ln_b, w_s, b_s, w_out, layer):
    t, d = x.shape
    e = ln_g.shape[0]
    tm = GMLP_ROWS
    return pl.pallas_call(
        _gmlp_kernel,
        grid=(t // tm,),
        in_specs=[
            pl.BlockSpec((tm, d), lambda i: (i, 0)),
            _resident((1, d)),
            _layer_resident(w_in.shape, layer),
            _resident((1, e)),
            _resident((1, e)),
            _layer_resident(w_s.shape, layer),
            _resident((CHUNK, GMLP_GROUPS)),
            _layer_resident(w_out.shape, layer),
        ],
        out_specs=pl.BlockSpec((tm, d), lambda i: (i, 0)),
        out_shape=jax.ShapeDtypeStruct((t, d), F32),
        scratch_shapes=[pltpu.VMEM((tm, e), BF16), pltpu.VMEM((tm, e), BF16)],
        compiler_params=_compiler_params(1),
        name="gmlp",
    )(x, g.reshape(1, d), w_in, ln_g.reshape(1, e), ln_b.reshape(1, e), w_s, b_s.T, w_out)


def _split3(x):
    hi = x.astype(BF16)
    r = x - hi.astype(F32)
    mid = r.astype(BF16)
    lo = (r - mid.astype(F32)).astype(BF16)
    return hi, mid, lo


def _fox_proj_kernel(x_ref, g_ref, w_ref, wf_ref, bf_ref, qg_ref, kg_ref, gmat_ref,
                     eqk_ref, oneq_ref, onek_ref,
                     q_ref, k_ref, v_ref, gate_ref, qa_ref, ka_ref, carry_ref,
                     *, blocks_per_seq):
    tm = x_ref.shape[0]
    wdt = q_ref.shape[1]
    i = pl.program_id(0)
    h = _rms_norm(x_ref[...], g_ref[...]).astype(BF16)

    gmat = gmat_ref[...]
    for base, gain_ref, dst in ((0, qg_ref, q_ref), (wdt, kg_ref, k_ref)):
        for s0 in range(0, wdt, PROJ_SLAB):
            y = _dot(h, w_ref[:, base + s0: base + s0 + PROJ_SLAB])
            for c0 in range(0, PROJ_SLAB, MXU_DIM):
                cols = slice(s0 + c0, s0 + c0 + MXU_DIM)
                yy = y[:, c0:c0 + MXU_DIM]
                ms = _dot((yy * yy).astype(BF16), gmat)
                dst[:, cols] = (yy * lax.rsqrt(ms + RMS_EPS) * gain_ref[:, cols]).astype(BF16)
    for s0 in range(0, wdt, PROJ_SLAB):
        cols = slice(s0, s0 + PROJ_SLAB)
        v_ref[:, cols] = _dot(h, w_ref[:, 2 * wdt + s0: 2 * wdt + s0 + PROJ_SLAB]).astype(BF16)
        gt = _dot(h, w_ref[:, 3 * wdt + s0: 3 * wdt + s0 + PROJ_SLAB])
        gate_ref[:, cols] = (1.0 / (1.0 + jnp.exp(-gt))).astype(BF16)

    f = _dot(h, wf_ref[...]) + bf_ref[...]
    logf = jnp.minimum(f, 0.0) - jnp.log(1.0 + jnp.exp(-jnp.abs(f)))
    row = lax.broadcasted_iota(jnp.int32, (tm, tm), 0)
    col = lax.broadcasted_iota(jnp.int32, (tm, tm), 1)
    tril = jnp.where(row >= col, 1.0, 0.0).astype(BF16)
    first = (i % blocks_per_seq) == 0
    carry = jnp.where(first, 0.0, carry_ref[0:1, :])
    csum = _dot(tril, jnp.concatenate(_split3(logf), axis=1))
    c = (csum[:, :LANES] + csum[:, LANES:2 * LANES]) + csum[:, 2 * LANES:] + carry
    carry_ref[0:1, :] = c[tm - 1:tm, :]

    pieces = jnp.concatenate(_split3(c * LOG2E), axis=1)
    slabs = _dot(pieces, eqk_ref[...])
    qa_ref[...] = (slabs[:, :LANES] + oneq_ref[...]).astype(BF16)
    ka_ref[...] = (onek_ref[...] - slabs[:, LANES:]).astype(BF16)


def _bias_slab_constants(n_heads):
    eqk = np.zeros((3 * LANES, 2 * LANES), np.float32)
    oneq = np.zeros((1, LANES), np.float32)
    onek = np.zeros((1, LANES), np.float32)
    for h in range(n_heads):
        for p in range(3):
            eqk[p * LANES + h, AUG_PER_HEAD * h + p] = 1.0
            eqk[p * LANES + h, LANES + AUG_PER_HEAD * h + 3 + p] = 1.0
            oneq[0, AUG_PER_HEAD * h + 3 + p] = 1.0
            onek[0, AUG_PER_HEAD * h + p] = 1.0
    return jnp.asarray(eqk, BF16), jnp.asarray(oneq), jnp.asarray(onek)


def _head_mean_matrix():
    idx = np.arange(MXU_DIM) // HEAD_DIM
    return jnp.asarray((idx[:, None] == idx[None, :]).astype(np.float32) / HEAD_DIM, BF16)


def _fox_proj(x, g, w_in, layer, w_f, b_f, q_gain, k_gain, seq):
    t, d = x.shape
    wdt = q_gain.shape[1]
    n_heads = wdt // HEAD_DIM
    tm = PROJ_ROWS
    eqk, oneq, onek = _bias_slab_constants(n_heads)
    gmat = _head_mean_matrix()
    row_spec = lambda width: pl.BlockSpec((tm, width), lambda i: (i, 0))
    bf = lambda width: jax.ShapeDtypeStruct((t, width), BF16)
    return pl.pallas_call(
        functools.partial(_fox_proj_kernel, blocks_per_seq=seq // tm),
        grid=(t // tm,),
        in_specs=[
            row_spec(d), _resident((1, d)), _layer_resident(w_in.shape, layer, width=4 * wdt),
            _resident(w_f.shape),
            _resident((1, LANES)), _resident((1, wdt)), _resident((1, wdt)),
            _resident(gmat.shape), _resident(eqk.shape),
            _resident((1, LANES)), _resident((1, LANES)),
        ],
        out_specs=[row_spec(wdt), row_spec(wdt), row_spec(wdt), row_spec(wdt),
                   row_spec(LANES), row_spec(LANES)],
        out_shape=[bf(wdt), bf(wdt), bf(wdt), bf(wdt), bf(LANES), bf(LANES)],
        scratch_shapes=[pltpu.VMEM((8, LANES), F32)],
        compiler_params=_compiler_params(1),
        name="fox_proj",
    )(x, g.reshape(1, d), w_in, w_f, b_f, q_gain, k_gain, gmat, eqk, oneq, onek)


def _nt_dot(a, b):
    return lax.dot_general(a, b, (((1,), (1,)), ((), ())), preferred_element_type=F32)


def _fox_attn_kernel(q_ref, qa_ref, k_ref, ka_ref, v_ref, gate_ref, o_ref, kcat_ref, vt_ref):
    seq = k_ref.shape[0]
    tq = ATT_Q
    p = pl.program_id(1)
    lane = lax.broadcasted_iota(jnp.int32, (1, LANES), 1)

    kcat_ref[:, :LANES] = k_ref[...]
    kcat_ref[:, LANES:] = ka_ref[...]
    vt_ref[:LANES, :] = v_ref[...].astype(F32).T.astype(BF16)
    vt_ref[LANES:, :] = jnp.ones((ONES_ROWS, seq), BF16)

    head_masks = []
    for e in range(2):
        in_half = (lane >= HEAD_DIM * e) & (lane < HEAD_DIM * (e + 1))
        aug_lo = AUG_PER_HEAD * (2 * p + e)
        in_aug = (lane >= aug_lo) & (lane < aug_lo + AUG_PER_HEAD)
        head_masks.append((jnp.where(in_half, 1.0, 0.0).astype(BF16),
                           jnp.where(in_aug, 1.0, 0.0).astype(BF16)))

    key_idx = lax.broadcasted_iota(jnp.int32, (tq, 2 * tq), 0)
    qry_idx = lax.broadcasted_iota(jnp.int32, (tq, 2 * tq), 1)
    visible = key_idx <= qry_idx - jnp.where(qry_idx >= tq, tq, 0)
    def scores(i):
        r0, r1 = i * tq, (i + 1) * tq
        q = q_ref[r0:r1, :]
        qa = qa_ref[r0:r1, :]
        qboth = jnp.concatenate(
            [jnp.concatenate([q * half, qa * aug], axis=1) for half, aug in head_masks], axis=0)
        s_diag = jnp.where(visible, _nt_dot(kcat_ref[r0:r1, :], qboth), NEG_BIG)
        m = jnp.max(s_diag, axis=0, keepdims=True)
        s_past = None
        if i > 0:
            s_past = _nt_dot(kcat_ref[:r0, :], qboth)
            m = jnp.maximum(m, jnp.max(s_past, axis=0, keepdims=True))
        return s_past, s_diag, m

    def finish(i, s_past, s_diag, m):
        r0, r1 = i * tq, (i + 1) * tq
        acc = _dot(vt_ref[:, r0:r1], jnp.exp2(s_diag - m).astype(BF16))
        if i > 0:
            acc = acc + _dot(vt_ref[:, :r0], jnp.exp2(s_past - m).astype(BF16))
        inv = 1.0 / acc[LANES:LANES + 1, :]
        o = jnp.concatenate([acc[:HEAD_DIM, :tq] * inv[:, :tq],
                             acc[HEAD_DIM:LANES, tq:] * inv[:, tq:]], axis=0).T
        o_ref[r0:r1, :] = (o * gate_ref[r0:r1, :].astype(F32)).astype(BF16)

    order = list(range(seq // tq - 1, -1, -1))
    ahead = 2
    pending = [scores(i) for i in order[:ahead]]
    for n, i in enumerate(order):
        if n + ahead < len(order):
            pending.append(scores(order[n + ahead]))
        finish(i, *pending.pop(0))


def _fox_attn(q, k, v, gate, qa, ka, batch, seq):
    t, wdt = q.shape
    pairs = wdt // LANES
    pair_spec = pl.BlockSpec((seq, LANES), lambda b, p: (b, p))
    aug_spec = pl.BlockSpec((seq, LANES), lambda b, p: (b, 0))
    return pl.pallas_call(
        _fox_attn_kernel,
        grid=(batch, pairs),
        in_specs=[pair_spec, aug_spec, pair_spec, aug_spec, pair_spec, pair_spec],
        out_specs=pair_spec,
        out_shape=jax.ShapeDtypeStruct((t, wdt), BF16),
        scratch_shapes=[pltpu.VMEM((seq, 2 * LANES), BF16), pltpu.VMEM((LANES + ONES_ROWS, seq), BF16)],
        compiler_params=_compiler_params(2),
        name="fox_attn",
    )(q, qa, k, ka, v, gate)


def kernel(x, gmlp_w_in, gmlp_ln_g, gmlp_ln_b, gmlp_w_s, gmlp_b_s, gmlp_w_out, fox_w_in, fox_b_f, fox_q_g, fox_k_g, fox_w_out, mix_norm_g, mlp_norm_g, mlp_w1, mlp_w2):
    batch, seq, d = x.shape
    depth = mix_norm_g.shape[0]
    wdt = fox_w_out.shape[1]
    n_heads = wdt // HEAD_DIM
    xs = x.reshape(batch * seq, d)

    gmlp_w_in, gmlp_w_out, fox_w_in_bf, fox_w_out, mlp_w1, mlp_w2 = (
        w.astype(BF16) for w in (gmlp_w_in, gmlp_w_out, fox_w_in, fox_w_out, mlp_w1, mlp_w2))

    for layer in range(depth):
        j = layer // 2
        if layer % 2 == 0:
            xs = _gmlp_layer(xs, mix_norm_g[layer], gmlp_w_in, gmlp_ln_g[j], gmlp_ln_b[j],
                             gmlp_w_s, gmlp_b_s[j], gmlp_w_out, j)
            xs = _mlp_layer(xs, mlp_norm_g[layer], mlp_w1, mlp_w2, layer)
        else:
            w_f = jnp.pad(fox_w_in[j, :, 4 * wdt:], ((0, 0), (0, LANES - n_heads))).astype(BF16)
            b_f = jnp.pad(fox_b_f[j], (0, LANES - n_heads)).reshape(1, LANES)
            q_gain = (jnp.tile(fox_q_g[j], n_heads) * (HEAD_DIM ** -0.5 * LOG2E)).reshape(1, wdt)
            k_gain = jnp.tile(fox_k_g[j], n_heads).reshape(1, wdt)
            q, k, v, gate, qa, ka = _fox_proj(xs, mix_norm_g[layer], fox_w_in_bf, j, w_f, b_f,
                                              q_gain, k_gain, seq)
            og = _fox_attn(q, k, v, gate, qa, ka, batch, seq)
            xs = _mlp_layer(xs, mlp_norm_g[layer], mlp_w1, mlp_w2, layer,
                            og=og, wo=fox_w_out, wo_layer=j)
    return xs.reshape(batch, seq, d)
```

```python
import functools
import math

import numpy as np
import jax
import jax.numpy as jnp
from jax import lax
from jax.experimental import pallas as pl
from jax.experimental.pallas import tpu as pltpu

F32 = jnp.float32
BF16 = jnp.bfloat16

CHUNK = 128
GMLP_GROUPS = 8
HEAD_DIM = 64
RMS_EPS = 1e-6
LN_EPS = 1e-5
LOG2E = math.log2(math.e)

LANES = 128
MXU_DIM = 256
VMEM_LIMIT_BYTES = 56 * 1024 * 1024

MLP_ROWS = 1024
MLP_FF_TILE = 512
GMLP_ROWS = 512
PROJ_ROWS = 512
PROJ_SLAB = 512
ATT_Q = 256
ATT_PAIRS = 2
ATT_AHEAD = 3
ONES_ROWS = 16
AUG_PER_HEAD = 6
NEG_BIG = -1e30


def _compiler_params(n_axes):
    return pltpu.CompilerParams(
        dimension_semantics=("arbitrary",) * n_axes,
        vmem_limit_bytes=VMEM_LIMIT_BYTES,
    )


def _resident(shape):
    nd = len(shape)
    return pl.BlockSpec(shape, lambda *_: (0,) * nd, pipeline_mode=pl.Buffered(1))


def _layer_resident(stacked_shape, layer, width=None):
    tail = tuple(stacked_shape[1:-1]) + (width or stacked_shape[-1],)
    return pl.BlockSpec((None,) + tail, lambda *_: (layer,) + (0,) * len(tail),
                        pipeline_mode=pl.Buffered(1))


def _rms_norm(x, g):
    ms = jnp.mean(x * x, axis=-1, keepdims=True)
    return x * lax.rsqrt(ms + RMS_EPS) * g


def _gelu_tanh(x):
    c = math.sqrt(2.0 / math.pi)
    return x * (0.5 * (1.0 + jnp.tanh(c * (x + 0.044715 * (x * x * x)))))


def _dot(a, b):
    return jnp.dot(a, b, preferred_element_type=F32)


def _mlp_kernel(*refs, fused_out_proj):
    if fused_out_proj:
        x_ref, og_ref, wo_ref, g_ref, w1_ref, w2_ref, o_ref = refs
    else:
        x_ref, g_ref, w1_ref, w2_ref, o_ref = refs
    tf = MLP_FF_TILE
    x = x_ref[...]
    if fused_out_proj:
        x = x + _dot(og_ref[...], wo_ref[...])
    h = _rms_norm(x, g_ref[...]).astype(BF16)
    acc = x
    for f0 in range(0, w1_ref.shape[1], tf):
        a = jnp.maximum(_dot(h, w1_ref[:, f0:f0 + tf]), 0.0)
        acc = acc + _dot((a * a).astype(BF16), w2_ref[f0:f0 + tf, :])
    o_ref[...] = acc


def _mlp_layer(x, g, w1, w2, layer, og=None, wo=None, wo_layer=None):
    t, d = x.shape
    tm = MLP_ROWS
    fused = og is not None
    row_spec = lambda width: pl.BlockSpec((tm, width), lambda i: (i, 0))
    in_specs = [row_spec(d)]
    args = [x]
    if fused:
        in_specs += [row_spec(og.shape[1]), _layer_resident(wo.shape, wo_layer)]
        args += [og, wo]
    in_specs += [_resident((1, d)), _layer_resident(w1.shape, layer), _layer_resident(w2.shape, layer)]
    args += [g.reshape(1, d), w1, w2]
    return pl.pallas_call(
        functools.partial(_mlp_kernel, fused_out_proj=fused),
        grid=(t // tm,),
        in_specs=in_specs,
        out_specs=row_spec(d),
        out_shape=jax.ShapeDtypeStruct((t, d), F32),
        compiler_params=_compiler_params(1),
        name="mlp_fused" if fused else "mlp",
    )(*args)


def _gmlp_kernel(x_ref, g_ref, win_ref, lng_ref, lnb_ref, ws_ref, bst_ref, wout_ref,
                 o_ref, vn_ref, gated_ref):
    tm = x_ref.shape[0]
    e = lng_ref.shape[1]
    gd = e // GMLP_GROUPS
    x = x_ref[...]
    h = _rms_norm(x, g_ref[...]).astype(BF16)

    v = _gelu_tanh(_dot(h, win_ref[:, e:]))
    u_all = _gelu_tanh(_dot(h, win_ref[:, :e]))
    mu = jnp.mean(v, axis=-1, keepdims=True)
    var = jnp.mean(v * v, axis=-1, keepdims=True) - mu * mu
    vn_ref[...] = ((v - mu) * lax.rsqrt(var + LN_EPS) * lng_ref[...] + lnb_ref[...]).astype(BF16)

    row = lax.broadcasted_iota(jnp.int32, (CHUNK, CHUNK), 0)
    col = lax.broadcasted_iota(jnp.int32, (CHUNK, CHUNK), 1)
    causal = row >= col
    for grp in range(GMLP_GROUPS):
        cols = slice(grp * gd, (grp + 1) * gd)
        w = jnp.where(causal, ws_ref[grp], 0.0).astype(BF16)
        bias = bst_ref[:, grp:grp + 1]
        for c in range(tm // CHUNK):
            rows = slice(c * CHUNK, (c + 1) * CHUNK)
            s = _dot(w, vn_ref[rows, cols]) + bias
            gated_ref[rows, cols] = (u_all[rows, cols] * s).astype(BF16)

    o_ref[...] = x + _dot(gated_ref[...], wout_ref[...])


def _gmlp_layer(x, g, w_in, ln_g, ln_b, w_s, b_s, w_out, layer):
    t, d = x.shape
    e = ln_g.shape[0]
    tm = GMLP_ROWS
    return pl.pallas_call(
        _gmlp_kernel,
        grid=(t // tm,),
        in_specs=[
            pl.BlockSpec((tm, d), lambda i: (i, 0)),
            _resident((1, d)),
            _layer_resident(w_in.shape, layer),
            _resident((1, e)),
            _resident((1, e)),
            _layer_resident(w_s.shape, layer),
            _resident((CHUNK, GMLP_GROUPS)),
            _layer_resident(w_out.shape, layer),
        ],
        out_specs=pl.BlockSpec((tm, d), lambda i: (i, 0)),
        out_shape=jax.ShapeDtypeStruct((t, d), F32),
        scratch_shapes=[pltpu.VMEM((tm, e), BF16), pltpu.VMEM((tm, e), BF16)],
        compiler_params=_compiler_params(1),
        name="gmlp",
    )(x, g.reshape(1, d), w_in, ln_g.reshape(1, e), ln_b.reshape(1, e), w_s, b_s.T, w_out)


def _split3(x):
    hi = x.astype(BF16)
    r = x - hi.astype(F32)
    mid = r.astype(BF16)
    lo = (r - mid.astype(F32)).astype(BF16)
    return hi, mid, lo


def _fox_proj_kernel(x_ref, g_ref, w_ref, wf_ref, bf_ref, qg_ref, kg_ref, gmat_ref,
                     eqk_ref, oneq_ref, onek_ref,
                     q_ref, k_ref, v_ref, gate_ref, qa_ref, ka_ref, carry_ref,
                     *, blocks_per_seq):
    tm = x_ref.shape[0]
    wdt = q_ref.shape[1]
    i = pl.program_id(0)
    h = _rms_norm(x_ref[...], g_ref[...]).astype(BF16)

    gmat = gmat_ref[...]

    def normed_heads(base, gain_ref, dst):
        for s0 in range(0, wdt, PROJ_SLAB):
            y = _dot(h, w_ref[:, base + s0: base + s0 + PROJ_SLAB])
            for c0 in range(0, PROJ_SLAB, MXU_DIM):
                cols = slice(s0 + c0, s0 + c0 + MXU_DIM)
                yy = y[:, c0:c0 + MXU_DIM]
                ms = _dot((yy * yy).astype(BF16), gmat)
                dst[:, cols] = (yy * lax.rsqrt(ms + RMS_EPS) * gain_ref[:, cols]).astype(BF16)

    f = _dot(h, wf_ref[...]) + bf_ref[...]
    logf = jnp.minimum(f, 0.0) - jnp.log(1.0 + jnp.exp(-jnp.abs(f)))
    normed_heads(0, qg_ref, q_ref)

    row = lax.broadcasted_iota(jnp.int32, (tm, tm), 0)
    col = lax.broadcasted_iota(jnp.int32, (tm, tm), 1)
    tril = jnp.where(row >= col, 1.0, 0.0).astype(BF16)
    first = (i % blocks_per_seq) == 0
    carry = jnp.where(first, 0.0, carry_ref[0:1, :])
    csum = _dot(tril, jnp.concatenate(_split3(logf), axis=1))
    c = (csum[:, :LANES] + csum[:, LANES:2 * LANES]) + csum[:, 2 * LANES:] + carry
    carry_ref[0:1, :] = c[tm - 1:tm, :]
    normed_heads(wdt, kg_ref, k_ref)

    pieces = jnp.concatenate(_split3(c * LOG2E), axis=1)
    slabs = _dot(pieces, eqk_ref[...])
    qa_ref[...] = (slabs[:, :LANES] + oneq_ref[...]).astype(BF16)
    ka_ref[...] = (onek_ref[...] - slabs[:, LANES:]).astype(BF16)

    for s0 in range(0, wdt, PROJ_SLAB):
        cols = slice(s0, s0 + PROJ_SLAB)
        v_ref[:, cols] = _dot(h, w_ref[:, 2 * wdt + s0: 2 * wdt + s0 + PROJ_SLAB]).astype(BF16)
        gt = _dot(h, w_ref[:, 3 * wdt + s0: 3 * wdt + s0 + PROJ_SLAB])
        gate_ref[:, cols] = (1.0 / (1.0 + jnp.exp(-gt))).astype(BF16)


def _bias_slab_constants(n_heads):
    eqk = np.zeros((3 * LANES, 2 * LANES), np.float32)
    oneq = np.zeros((1, LANES), np.float32)
    onek = np.zeros((1, LANES), np.float32)
    for h in range(n_heads):
        for p in range(3):
            eqk[p * LANES + h, AUG_PER_HEAD * h + p] = 1.0
            eqk[p * LANES + h, LANES + AUG_PER_HEAD * h + 3 + p] = 1.0
            oneq[0, AUG_PER_HEAD * h + 3 + p] = 1.0
            onek[0, AUG_PER_HEAD * h + p] = 1.0
    return jnp.asarray(eqk, BF16), jnp.asarray(oneq), jnp.asarray(onek)


def _head_mean_matrix():
    idx = np.arange(MXU_DIM) // HEAD_DIM
    return jnp.asarray((idx[:, None] == idx[None, :]).astype(np.float32) / HEAD_DIM, BF16)


def _fox_proj(x, g, w_in, layer, w_f, b_f, q_gain, k_gain, seq):
    t, d = x.shape
    wdt = q_gain.shape[1]
    n_heads = wdt // HEAD_DIM
    tm = PROJ_ROWS
    eqk, oneq, onek = _bias_slab_constants(n_heads)
    gmat = _head_mean_matrix()
    row_spec = lambda width: pl.BlockSpec((tm, width), lambda i: (i, 0))
    bf = lambda width: jax.ShapeDtypeStruct((t, width), BF16)
    return pl.pallas_call(
        functools.partial(_fox_proj_kernel, blocks_per_seq=seq // tm),
        grid=(t // tm,),
        in_specs=[
            row_spec(d), _resident((1, d)), _layer_resident(w_in.shape, layer, width=4 * wdt),
            _resident(w_f.shape),
            _resident((1, LANES)), _resident((1, wdt)), _resident((1, wdt)),
            _resident(gmat.shape), _resident(eqk.shape),
            _resident((1, LANES)), _resident((1, LANES)),
        ],
        out_specs=[row_spec(wdt), row_spec(wdt), row_spec(wdt), row_spec(wdt),
                   row_spec(LANES), row_spec(LANES)],
        out_shape=[bf(wdt), bf(wdt), bf(wdt), bf(wdt), bf(LANES), bf(LANES)],
        scratch_shapes=[pltpu.VMEM((8, LANES), F32)],
        compiler_params=_compiler_params(1),
        name="fox_proj",
    )(x, g.reshape(1, d), w_in, w_f, b_f, q_gain, k_gain, gmat, eqk, oneq, onek)


def _nt_dot(a, b):
    return lax.dot_general(a, b, (((1,), (1,)), ((), ())), preferred_element_type=F32)


def _fox_attn_kernel(q_ref, qa_ref, k_ref, ka_ref, v_ref, gate_ref, o_ref, kcat_ref, vt_ref):
    seq = k_ref.shape[0]
    tq = ATT_Q
    group = pl.program_id(1)
    lane = lax.broadcasted_iota(jnp.int32, (1, LANES), 1)

    key_idx = lax.broadcasted_iota(jnp.int32, (tq, 2 * tq), 0)
    qry_idx = lax.broadcasted_iota(jnp.int32, (tq, 2 * tq), 1)
    visible = key_idx <= qry_idx - jnp.where(qry_idx >= tq, tq, 0)

    head_masks = []
    for pp in range(ATT_PAIRS):
        lanes = slice(pp * LANES, (pp + 1) * LANES)
        kcat_ref[pp, :, :LANES] = k_ref[:, lanes]
        kcat_ref[pp, :, LANES:] = ka_ref[...]
        vt_ref[pp, :LANES, :] = v_ref[:, lanes].astype(F32).T.astype(BF16)
        vt_ref[pp, LANES:, :] = jnp.ones((ONES_ROWS, seq), BF16)
        masks = []
        for e in range(2):
            in_half = (lane >= HEAD_DIM * e) & (lane < HEAD_DIM * (e + 1))
            aug_lo = AUG_PER_HEAD * (2 * (ATT_PAIRS * group + pp) + e)
            in_aug = (lane >= aug_lo) & (lane < aug_lo + AUG_PER_HEAD)
            masks.append((jnp.where(in_half, 1.0, 0.0).astype(BF16),
                          jnp.where(in_aug, 1.0, 0.0).astype(BF16)))
        head_masks.append(masks)

    def scores(pp, i):
        r0, r1 = i * tq, (i + 1) * tq
        q = q_ref[r0:r1, pp * LANES:(pp + 1) * LANES]
        qa = qa_ref[r0:r1, :]
        qboth = jnp.concatenate(
            [jnp.concatenate([q * half, qa * aug], axis=1) for half, aug in head_masks[pp]], axis=0)
        s_diag = jnp.where(visible, _nt_dot(kcat_ref[pp, r0:r1, :], qboth), NEG_BIG)
        m = jnp.max(s_diag, axis=0, keepdims=True)
        s_past = None
        if i > 0:
            s_past = _nt_dot(kcat_ref[pp, :r0, :], qboth)
            m = jnp.maximum(m, jnp.max(s_past, axis=0, keepdims=True))
        return s_past, s_diag, m

    def finish(pp, i, s_past, s_diag, m):
        r0, r1 = i * tq, (i + 1) * tq
        lanes = slice(pp * LANES, (pp + 1) * LANES)
        acc = _dot(vt_ref[pp, :, r0:r1], jnp.exp2(s_diag - m).astype(BF16))
        if i > 0:
            acc = acc + _dot(vt_ref[pp, :, :r0], jnp.exp2(s_past - m).astype(BF16))
        inv = 1.0 / acc[LANES:LANES + 1, :]
        o = jnp.concatenate([acc[:HEAD_DIM, :tq] * inv[:, :tq],
                             acc[HEAD_DIM:LANES, tq:] * inv[:, tq:]], axis=0).T
        o_ref[r0:r1, lanes] = (o * gate_ref[r0:r1, lanes].astype(F32)).astype(BF16)

    order = [(pp, i) for pp in range(ATT_PAIRS) for i in range(seq // tq - 1, -1, -1)]
    pending = [scores(*blk) for blk in order[:ATT_AHEAD]]
    for n, blk in enumerate(order):
        if n + ATT_AHEAD < len(order):
            pending.append(scores(*order[n + ATT_AHEAD]))
        finish(*blk, *pending.pop(0))


def _fox_attn(q, k, v, gate, qa, ka, batch, seq):
    t, wdt = q.shape
    width = ATT_PAIRS * LANES
    pair_spec = pl.BlockSpec((seq, width), lambda b, p: (b, p))
    aug_spec = pl.BlockSpec((seq, LANES), lambda b, p: (b, 0))
    return pl.pallas_call(
        _fox_attn_kernel,
        grid=(batch, wdt // width),
        in_specs=[pair_spec, aug_spec, pair_spec, aug_spec, pair_spec, pair_spec],
        out_specs=pair_spec,
        out_shape=jax.ShapeDtypeStruct((t, wdt), BF16),
        scratch_shapes=[pltpu.VMEM((ATT_PAIRS, seq, 2 * LANES), BF16),
                        pltpu.VMEM((ATT_PAIRS, LANES + ONES_ROWS, seq), BF16)],
        compiler_params=_compiler_params(2),
        name="fox_attn",
    )(q, qa, k, ka, v, gate)


def kernel(x, gmlp_w_in, gmlp_ln_g, gmlp_ln_b, gmlp_w_s, gmlp_b_s, gmlp_w_out, fox_w_in, fox_b_f, fox_q_g, fox_k_g, fox_w_out, mix_norm_g, mlp_norm_g, mlp_w1, mlp_w2):
    batch, seq, d = x.shape
    depth = mix_norm_g.shape[0]
    wdt = fox_w_out.shape[1]
    n_heads = wdt // HEAD_DIM
    xs = x.reshape(batch * seq, d)

    gmlp_w_in, gmlp_w_out, fox_w_in_bf, fox_w_out, mlp_w1, mlp_w2 = (
        w.astype(BF16) for w in (gmlp_w_in, gmlp_w_out, fox_w_in, fox_w_out, mlp_w1, mlp_w2))

    for layer in range(depth):
        j = layer // 2
        if layer % 2 == 0:
            xs = _gmlp_layer(xs, mix_norm_g[layer], gmlp_w_in, gmlp_ln_g[j], gmlp_ln_b[j],
                             gmlp_w_s, gmlp_b_s[j], gmlp_w_out, j)
            xs = _mlp_layer(xs, mlp_norm_g[layer], mlp_w1, mlp_w2, layer)
        else:
            w_f = jnp.pad(fox_w_in[j, :, 4 * wdt:], ((0, 0), (0, LANES - n_heads))).astype(BF16)
            b_f = jnp.pad(fox_b_f[j], (0, LANES - n_heads)).reshape(1, LANES)
            q_gain = (jnp.tile(fox_q_g[j], n_heads) * (HEAD_DIM ** -0.5 * LOG2E)).reshape(1, wdt)
            k_gain = jnp.tile(fox_k_g[j], n_heads).reshape(1, wdt)
            q, k, v, gate, qa, ka = _fox_proj(xs, mix_norm_g[layer], fox_w_in_bf, j, w_f, b_f,
                                              q_gain, k_gain, seq)
            og = _fox_attn(q, k, v, gate, qa, ka, batch, seq)
            xs = _mlp_layer(xs, mlp_norm_g[layer], mlp_w1, mlp_w2, layer,
                            og=og, wo=fox_w_out, wo_layer=j)
    return xs.reshape(batch, seq, d)
```

```python
import functools
import math

import numpy as np
import jax
import jax.numpy as jnp
from jax import lax
from jax.experimental import pallas as pl
from jax.experimental.pallas import tpu as pltpu

F32 = jnp.float32
BF16 = jnp.bfloat16

CHUNK = 128
GMLP_GROUPS = 8
HEAD_DIM = 64
RMS_EPS = 1e-6
LN_EPS = 1e-5
LOG2E = math.log2(math.e)

LANES = 128
MXU_DIM = 256
VMEM_LIMIT_BYTES = 56 * 1024 * 1024

MLP_ROWS = 1024
MLP_FF_TILE = 512
GMLP_ROWS = 512
PROJ_ROWS = 512
PROJ_SLAB = 512
ATT_Q = 256
ATT_PAIRS = 2
ATT_AHEAD = 3
ONES_ROWS = 16
AUG_PER_HEAD = 6
NEG_BIG = -1e30


def _compiler_params(n_axes):
    return pltpu.CompilerParams(
        dimension_semantics=("arbitrary",) * n_axes,
        vmem_limit_bytes=VMEM_LIMIT_BYTES,
    )


def _resident(shape):
    nd = len(shape)
    return pl.BlockSpec(shape, lambda *_: (0,) * nd, pipeline_mode=pl.Buffered(1))


def _layer_resident(stacked_shape, layer, width=None):
    tail = tuple(stacked_shape[1:-1]) + (width or stacked_shape[-1],)
    return pl.BlockSpec((None,) + tail, lambda *_: (layer,) + (0,) * len(tail),
                        pipeline_mode=pl.Buffered(1))


def _rms_norm(x, g):
    ms = jnp.mean(x * x, axis=-1, keepdims=True)
    return x * lax.rsqrt(ms + RMS_EPS) * g


def _gelu_tanh(x):
    c = math.sqrt(2.0 / math.pi)
    return x * (0.5 * (1.0 + jnp.tanh(c * (x + 0.044715 * (x * x * x)))))


def _dot(a, b):
    return jnp.dot(a, b, preferred_element_type=F32)


def _mlp_kernel(*refs, fused_out_proj):
    if fused_out_proj:
        x_ref, og_ref, wo_ref, g_ref, w1_ref, w2_ref, o_ref = refs
    else:
        x_ref, g_ref, w1_ref, w2_ref, o_ref = refs
    tf = MLP_FF_TILE
    x = x_ref[...]
    if fused_out_proj:
        og = jnp.concatenate([og_ref[p] for p in range(og_ref.shape[0])], axis=1)
        x = x + _dot(og, wo_ref[...])
    h = _rms_norm(x, g_ref[...]).astype(BF16)
    acc = x
    for f0 in range(0, w1_ref.shape[1], tf):
        a = jnp.maximum(_dot(h, w1_ref[:, f0:f0 + tf]), 0.0)
        acc = acc + _dot((a * a).astype(BF16), w2_ref[f0:f0 + tf, :])
    o_ref[...] = acc


def _mlp_layer(x, g, w1, w2, layer, og=None, wo=None, wo_layer=None):
    t, d = x.shape
    tm = MLP_ROWS
    fused = og is not None
    row_spec = lambda width: pl.BlockSpec((tm, width), lambda i: (i, 0))
    in_specs = [row_spec(d)]
    args = [x]
    if fused:
        in_specs += [pl.BlockSpec((og.shape[0], tm, LANES), lambda i: (0, i, 0)),
                     _layer_resident(wo.shape, wo_layer)]
        args += [og, wo]
    in_specs += [_resident((1, d)), _layer_resident(w1.shape, layer), _layer_resident(w2.shape, layer)]
    args += [g.reshape(1, d), w1, w2]
    return pl.pallas_call(
        functools.partial(_mlp_kernel, fused_out_proj=fused),
        grid=(t // tm,),
        in_specs=in_specs,
        out_specs=row_spec(d),
        out_shape=jax.ShapeDtypeStruct((t, d), F32),
        compiler_params=_compiler_params(1),
        name="mlp_fused" if fused else "mlp",
    )(*args)


def _gmlp_kernel(x_ref, g_ref, win_ref, lng_ref, lnb_ref, ws_ref, bst_ref, wout_ref,
                 o_ref, vn_ref, gated_ref):
    tm = x_ref.shape[0]
    e = lng_ref.shape[1]
    gd = e // GMLP_GROUPS
    x = x_ref[...]
    h = _rms_norm(x, g_ref[...]).astype(BF16)

    v = _gelu_tanh(_dot(h, win_ref[:, e:]))
    u_all = _gelu_tanh(_dot(h, win_ref[:, :e]))
    mu = jnp.mean(v, axis=-1, keepdims=True)
    var = jnp.mean(v * v, axis=-1, keepdims=True) - mu * mu
    vn_ref[...] = ((v - mu) * lax.rsqrt(var + LN_EPS) * lng_ref[...] + lnb_ref[...]).astype(BF16)

    row = lax.broadcasted_iota(jnp.int32, (CHUNK, CHUNK), 0)
    col = lax.broadcasted_iota(jnp.int32, (CHUNK, CHUNK), 1)
    causal = row >= col
    for grp in range(GMLP_GROUPS):
        cols = slice(grp * gd, (grp + 1) * gd)
        w = jnp.where(causal, ws_ref[grp], 0.0).astype(BF16)
        bias = bst_ref[:, grp:grp + 1]
        for c in range(tm // CHUNK):
            rows = slice(c * CHUNK, (c + 1) * CHUNK)
            s = _dot(w, vn_ref[rows, cols]) + bias
            gated_ref[rows, cols] = (u_all[rows, cols] * s).astype(BF16)

    o_ref[...] = x + _dot(gated_ref[...], wout_ref[...])


def _gmlp_layer(x, g, w_in, ln_g, ln_b, w_s, b_s, w_out, layer):
    t, d = x.shape
    e = ln_g.shape[0]
    tm = GMLP_ROWS
    return pl.pallas_call(
        _gmlp_kernel,
        grid=(t // tm,),
        in_specs=[
            pl.BlockSpec((tm, d), lambda i: (i, 0)),
            _resident((1, d)),
            _layer_resident(w_in.shape, layer),
            _resident((1, e)),
            _resident((1, e)),
            _layer_resident(w_s.shape, layer),
            _resident((CHUNK, GMLP_GROUPS)),
            _layer_resident(w_out.shape, layer),
        ],
        out_specs=pl.BlockSpec((tm, d), lambda i: (i, 0)),
        out_shape=jax.ShapeDtypeStruct((t, d), F32),
        scratch_shapes=[pltpu.VMEM((tm, e), BF16), pltpu.VMEM((tm, e), BF16)],
        compiler_params=_compiler_params(1),
        name="gmlp",
    )(x, g.reshape(1, d), w_in, ln_g.reshape(1, e), ln_b.reshape(1, e), w_s, b_s.T, w_out)


def _split3(x):
    hi = x.astype(BF16)
    r = x - hi.astype(F32)
    mid = r.astype(BF16)
    lo = (r - mid.astype(F32)).astype(BF16)
    return hi, mid, lo


def _store_pairs(dst_ref, col0, val):
    for c in range(0, val.shape[1], LANES):
        dst_ref[(col0 + c) // LANES] = val[:, c:c + LANES]


def _fox_proj_kernel(x_ref, g_ref, w_ref, wf_ref, bf_ref, qg_ref, kg_ref, gmat_ref,
                     eqk_ref, oneq_ref, onek_ref,
                     q_ref, k_ref, v_ref, gate_ref, qa_ref, ka_ref, carry_ref,
                     *, blocks_per_seq):
    tm = x_ref.shape[0]
    wdt = qg_ref.shape[1]
    i = pl.program_id(0)
    h = _rms_norm(x_ref[...], g_ref[...]).astype(BF16)

    gmat = gmat_ref[...]

    def normed_heads(base, gain_ref, dst):
        for s0 in range(0, wdt, PROJ_SLAB):
            y = _dot(h, w_ref[:, base + s0: base + s0 + PROJ_SLAB])
            for c0 in range(0, PROJ_SLAB, MXU_DIM):
                cols = slice(s0 + c0, s0 + c0 + MXU_DIM)
                yy = y[:, c0:c0 + MXU_DIM]
                ms = _dot((yy * yy).astype(BF16), gmat)
                _store_pairs(dst, s0 + c0, (yy * lax.rsqrt(ms + RMS_EPS) * gain_ref[:, cols]).astype(BF16))

    f = _dot(h, wf_ref[...]) + bf_ref[...]
    logf = jnp.minimum(f, 0.0) - jnp.log(1.0 + jnp.exp(-jnp.abs(f)))
    normed_heads(0, qg_ref, q_ref)

    row = lax.broadcasted_iota(jnp.int32, (tm, tm), 0)
    col = lax.broadcasted_iota(jnp.int32, (tm, tm), 1)
    tril = jnp.where(row >= col, 1.0, 0.0).astype(BF16)
    first = (i % blocks_per_seq) == 0
    carry = jnp.where(first, 0.0, carry_ref[0:1, :])
    csum = _dot(tril, jnp.concatenate(_split3(logf), axis=1))
    c = (csum[:, :LANES] + csum[:, LANES:2 * LANES]) + csum[:, 2 * LANES:] + carry
    carry_ref[0:1, :] = c[tm - 1:tm, :]
    normed_heads(wdt, kg_ref, k_ref)

    pieces = jnp.concatenate(_split3(c * LOG2E), axis=1)
    slabs = _dot(pieces, eqk_ref[...])
    qa_ref[...] = (slabs[:, :LANES] + oneq_ref[...]).astype(BF16)
    ka_ref[...] = (onek_ref[...] - slabs[:, LANES:]).astype(BF16)

    for s0 in range(0, wdt, PROJ_SLAB):
        _store_pairs(v_ref, s0, _dot(h, w_ref[:, 2 * wdt + s0: 2 * wdt + s0 + PROJ_SLAB]).astype(BF16))
        gt = _dot(h, w_ref[:, 3 * wdt + s0: 3 * wdt + s0 + PROJ_SLAB])
        _store_pairs(gate_ref, s0, (1.0 / (1.0 + jnp.exp(-gt))).astype(BF16))


def _bias_slab_constants(n_heads):
    eqk = np.zeros((3 * LANES, 2 * LANES), np.float32)
    oneq = np.zeros((1, LANES), np.float32)
    onek = np.zeros((1, LANES), np.float32)
    for h in range(n_heads):
        for p in range(3):
            eqk[p * LANES + h, AUG_PER_HEAD * h + p] = 1.0
            eqk[p * LANES + h, LANES + AUG_PER_HEAD * h + 3 + p] = 1.0
            oneq[0, AUG_PER_HEAD * h + 3 + p] = 1.0
            onek[0, AUG_PER_HEAD * h + p] = 1.0
    return jnp.asarray(eqk, BF16), jnp.asarray(oneq), jnp.asarray(onek)


def _head_mean_matrix():
    idx = np.arange(MXU_DIM) // HEAD_DIM
    return jnp.asarray((idx[:, None] == idx[None, :]).astype(np.float32) / HEAD_DIM, BF16)


def _fox_proj(x, g, w_in, layer, w_f, b_f, q_gain, k_gain, seq):
    t, d = x.shape
    wdt = q_gain.shape[1]
    n_heads = wdt // HEAD_DIM
    tm = PROJ_ROWS
    eqk, oneq, onek = _bias_slab_constants(n_heads)
    gmat = _head_mean_matrix()
    row_spec = lambda width: pl.BlockSpec((tm, width), lambda i: (i, 0))
    bf = lambda width: jax.ShapeDtypeStruct((t, width), BF16)
    pairs = wdt // LANES
    pair_spec = pl.BlockSpec((pairs, tm, LANES), lambda i: (0, i, 0))
    pair_shape = jax.ShapeDtypeStruct((pairs, t, LANES), BF16)
    return pl.pallas_call(
        functools.partial(_fox_proj_kernel, blocks_per_seq=seq // tm),
        grid=(t // tm,),
        in_specs=[
            row_spec(d), _resident((1, d)), _layer_resident(w_in.shape, layer, width=4 * wdt),
            _resident(w_f.shape),
            _resident((1, LANES)), _resident((1, wdt)), _resident((1, wdt)),
            _resident(gmat.shape), _resident(eqk.shape),
            _resident((1, LANES)), _resident((1, LANES)),
        ],
        out_specs=[pair_spec] * 4 + [row_spec(LANES), row_spec(LANES)],
        out_shape=[pair_shape] * 4 + [bf(LANES), bf(LANES)],
        scratch_shapes=[pltpu.VMEM((8, LANES), F32)],
        compiler_params=_compiler_params(1),
        name="fox_proj",
    )(x, g.reshape(1, d), w_in, w_f, b_f, q_gain, k_gain, gmat, eqk, oneq, onek)


def _nt_dot(a, b):
    return lax.dot_general(a, b, (((1,), (1,)), ((), ())), preferred_element_type=F32)


def _fox_attn_kernel(q_ref, qa_ref, k_ref, ka_ref, v_ref, gate_ref, o_ref, kcat_ref, vt_ref):
    seq = k_ref.shape[1]
    tq = ATT_Q
    group = pl.program_id(1)
    lane = lax.broadcasted_iota(jnp.int32, (1, LANES), 1)

    key_idx = lax.broadcasted_iota(jnp.int32, (tq, 2 * tq), 0)
    qry_idx = lax.broadcasted_iota(jnp.int32, (tq, 2 * tq), 1)
    visible = key_idx <= qry_idx - jnp.where(qry_idx >= tq, tq, 0)

    head_masks = []
    for pp in range(ATT_PAIRS):
        kcat_ref[pp, :, :LANES] = k_ref[pp]
        kcat_ref[pp, :, LANES:] = ka_ref[...]
        vt_ref[pp, :LANES, :] = v_ref[pp].astype(F32).T.astype(BF16)
        vt_ref[pp, LANES:, :] = jnp.ones((ONES_ROWS, seq), BF16)
        masks = []
        for e in range(2):
            in_half = (lane >= HEAD_DIM * e) & (lane < HEAD_DIM * (e + 1))
            aug_lo = AUG_PER_HEAD * (2 * (ATT_PAIRS * group + pp) + e)
            in_aug = (lane >= aug_lo) & (lane < aug_lo + AUG_PER_HEAD)
            masks.append((jnp.where(in_half, 1.0, 0.0).astype(BF16),
                          jnp.where(in_aug, 1.0, 0.0).astype(BF16)))
        head_masks.append(masks)

    def scores(pp, i):
        r0, r1 = i * tq, (i + 1) * tq
        q = q_ref[pp, r0:r1, :]
        qa = qa_ref[r0:r1, :]
        qboth = jnp.concatenate(
            [jnp.concatenate([q * half, qa * aug], axis=1) for half, aug in head_masks[pp]], axis=0)
        s_diag = jnp.where(visible, _nt_dot(kcat_ref[pp, r0:r1, :], qboth), NEG_BIG)
        m = jnp.max(s_diag, axis=0, keepdims=True)
        s_past = None
        if i > 0:
            s_past = _nt_dot(kcat_ref[pp, :r0, :], qboth)
            m = jnp.maximum(m, jnp.max(s_past, axis=0, keepdims=True))
        return s_past, s_diag, m

    def finish(pp, i, s_past, s_diag, m):
        r0, r1 = i * tq, (i + 1) * tq
        acc = _dot(vt_ref[pp, :, r0:r1], jnp.exp2(s_diag - m).astype(BF16))
        if i > 0:
            acc = acc + _dot(vt_ref[pp, :, :r0], jnp.exp2(s_past - m).astype(BF16))
        inv = 1.0 / acc[LANES:LANES + 1, :]
        o = jnp.concatenate([acc[:HEAD_DIM, :tq] * inv[:, :tq],
                             acc[HEAD_DIM:LANES, tq:] * inv[:, tq:]], axis=0).T
        o_ref[pp, r0:r1, :] = (o * gate_ref[pp, r0:r1, :].astype(F32)).astype(BF16)

    order = [(pp, i) for pp in range(ATT_PAIRS) for i in range(seq // tq - 1, -1, -1)]
    pending = [scores(*blk) for blk in order[:ATT_AHEAD]]
    for n, blk in enumerate(order):
        if n + ATT_AHEAD < len(order):
            pending.append(scores(*order[n + ATT_AHEAD]))
        finish(*blk, *pending.pop(0))


def _fox_attn(q, k, v, gate, qa, ka, batch, seq):
    pairs, t, _ = q.shape
    pair_spec = pl.BlockSpec((ATT_PAIRS, seq, LANES), lambda b, p: (p, b, 0))
    aug_spec = pl.BlockSpec((seq, LANES), lambda b, p: (b, 0))
    return pl.pallas_call(
        _fox_attn_kernel,
        grid=(batch, pairs // ATT_PAIRS),
        in_specs=[pair_spec, aug_spec, pair_spec, aug_spec, pair_spec, pair_spec],
        out_specs=pair_spec,
        out_shape=jax.ShapeDtypeStruct((pairs, t, LANES), BF16),
        scratch_shapes=[pltpu.VMEM((ATT_PAIRS, seq, 2 * LANES), BF16),
                        pltpu.VMEM((ATT_PAIRS, LANES + ONES_ROWS, seq), BF16)],
        compiler_params=_compiler_params(2),
        name="fox_attn",
    )(q, qa, k, ka, v, gate)


def kernel(x, gmlp_w_in, gmlp_ln_g, gmlp_ln_b, gmlp_w_s, gmlp_b_s, gmlp_w_out, fox_w_in, fox_b_f, fox_q_g, fox_k_g, fox_w_out, mix_norm_g, mlp_norm_g, mlp_w1, mlp_w2):
    batch, seq, d = x.shape
    depth = mix_norm_g.shape[0]
    wdt = fox_w_out.shape[1]
    n_heads = wdt // HEAD_DIM
    xs = x.reshape(batch * seq, d)

    gmlp_w_in, gmlp_w_out, fox_w_in_bf, fox_w_out, mlp_w1, mlp_w2 = (
        w.astype(BF16) for w in (gmlp_w_in, gmlp_w_out, fox_w_in, fox_w_out, mlp_w1, mlp_w2))

    for layer in range(depth):
        j = layer // 2
        if layer % 2 == 0:
            xs = _gmlp_layer(xs, mix_norm_g[layer], gmlp_w_in, gmlp_ln_g[j], gmlp_ln_b[j],
                             gmlp_w_s, gmlp_b_s[j], gmlp_w_out, j)
            xs = _mlp_layer(xs, mlp_norm_g[layer], mlp_w1, mlp_w2, layer)
        else:
            w_f = jnp.pad(fox_w_in[j, :, 4 * wdt:], ((0, 0), (0, LANES - n_heads))).astype(BF16)
            b_f = jnp.pad(fox_b_f[j], (0, LANES - n_heads)).reshape(1, LANES)
            q_gain = (jnp.tile(fox_q_g[j], n_heads) * (HEAD_DIM ** -0.5 * LOG2E)).reshape(1, wdt)
            k_gain = jnp.tile(fox_k_g[j], n_heads).reshape(1, wdt)
            q, k, v, gate, qa, ka = _fox_proj(xs, mix_norm_g[layer], fox_w_in_bf, j, w_f, b_f,
                                              q_gain, k_gain, seq)
            og = _fox_attn(q, k, v, gate, qa, ka, batch, seq)
            xs = _mlp_layer(xs, mlp_norm_g[layer], mlp_w1, mlp_w2, layer,
                            og=og, wo=fox_w_out, wo_layer=j)
    return xs.reshape(batch, seq, d)
```

```python
import functools
import math

import numpy as np
import jax
import jax.numpy as jnp
from jax import lax
from jax.experimental import pallas as pl
from jax.experimental.pallas import tpu as pltpu

F32 = jnp.float32
BF16 = jnp.bfloat16

CHUNK = 128
GMLP_GROUPS = 8
HEAD_DIM = 64
RMS_EPS = 1e-6
LN_EPS = 1e-5
LOG2E = math.log2(math.e)

LANES = 128
MXU_DIM = 256
VMEM_LIMIT_BYTES = 56 * 1024 * 1024

MLP_ROWS = 1024
MLP_FF_TILE = 512
GMLP_ROWS = 512
PROJ_ROWS = 512
PROJ_SLAB = 512
ATT_Q = 256
ATT_PAIRS = 2
ATT_AHEAD = 3
ONES_ROWS = 16
AUG_PER_HEAD = 6
SHIFT_LANES = 3
ATT_MAX_FIXED_SHIFT = 48.0
NEG_BIG = -1e30


def _compiler_params(n_axes):
    return pltpu.CompilerParams(
        dimension_semantics=("arbitrary",) * n_axes,
        vmem_limit_bytes=VMEM_LIMIT_BYTES,
    )


def _resident(shape):
    nd = len(shape)
    return pl.BlockSpec(shape, lambda *_: (0,) * nd, pipeline_mode=pl.Buffered(1))


def _layer_resident(stacked_shape, layer, width=None):
    tail = tuple(stacked_shape[1:-1]) + (width or stacked_shape[-1],)
    return pl.BlockSpec((None,) + tail, lambda *_: (layer,) + (0,) * len(tail),
                        pipeline_mode=pl.Buffered(1))


def _rms_norm(x, g):
    ms = jnp.mean(x * x, axis=-1, keepdims=True)
    return x * lax.rsqrt(ms + RMS_EPS) * g


def _gelu_tanh(x):
    c = math.sqrt(2.0 / math.pi)
    return x * (0.5 * (1.0 + jnp.tanh(c * (x + 0.044715 * (x * x * x)))))


def _dot(a, b):
    return jnp.dot(a, b, preferred_element_type=F32)


def _mlp_kernel(*refs, fused_out_proj):
    if fused_out_proj:
        x_ref, og_ref, wo_ref, g_ref, w1_ref, w2_ref, o_ref = refs
    else:
        x_ref, g_ref, w1_ref, w2_ref, o_ref = refs
    tf = MLP_FF_TILE
    x = x_ref[...]
    if fused_out_proj:
        og = jnp.concatenate([og_ref[p] for p in range(og_ref.shape[0])], axis=1)
        x = x + _dot(og, wo_ref[...])
    h = _rms_norm(x, g_ref[...]).astype(BF16)
    acc = x
    for f0 in range(0, w1_ref.shape[1], tf):
        a = jnp.maximum(_dot(h, w1_ref[:, f0:f0 + tf]), 0.0)
        acc = acc + _dot((a * a).astype(BF16), w2_ref[f0:f0 + tf, :])
    o_ref[...] = acc


def _mlp_layer(x, g, w1, w2, layer, og=None, wo=None, wo_layer=None):
    t, d = x.shape
    tm = MLP_ROWS
    fused = og is not None
    row_spec = lambda width: pl.BlockSpec((tm, width), lambda i: (i, 0))
    in_specs = [row_spec(d)]
    args = [x]
    if fused:
        in_specs += [pl.BlockSpec((og.shape[0], tm, LANES), lambda i: (0, i, 0)),
                     _layer_resident(wo.shape, wo_layer)]
        args += [og, wo]
    in_specs += [_resident((1, d)), _layer_resident(w1.shape, layer), _layer_resident(w2.shape, layer)]
    args += [g.reshape(1, d), w1, w2]
    return pl.pallas_call(
        functools.partial(_mlp_kernel, fused_out_proj=fused),
        grid=(t // tm,),
        in_specs=in_specs,
        out_specs=row_spec(d),
        out_shape=jax.ShapeDtypeStruct((t, d), F32),
        compiler_params=_compiler_params(1),
        name="mlp_fused" if fused else "mlp",
    )(*args)


def _gmlp_kernel(x_ref, g_ref, win_ref, lng_ref, lnb_ref, ws_ref, bst_ref, wout_ref,
                 o_ref, vn_ref, gated_ref):
    tm = x_ref.shape[0]
    e = lng_ref.shape[1]
    gd = e // GMLP_GROUPS
    x = x_ref[...]
    h = _rms_norm(x, g_ref[...]).astype(BF16)

    v = _gelu_tanh(_dot(h, win_ref[:, e:]))
    u_all = _gelu_tanh(_dot(h, win_ref[:, :e]))
    mu = jnp.mean(v, axis=-1, keepdims=True)
    var = jnp.mean(v * v, axis=-1, keepdims=True) - mu * mu
    vn_ref[...] = ((v - mu) * lax.rsqrt(var + LN_EPS) * lng_ref[...] + lnb_ref[...]).astype(BF16)

    row = lax.broadcasted_iota(jnp.int32, (CHUNK, CHUNK), 0)
    col = lax.broadcasted_iota(jnp.int32, (CHUNK, CHUNK), 1)
    causal = row >= col
    for grp in range(GMLP_GROUPS):
        cols = slice(grp * gd, (grp + 1) * gd)
        w = jnp.where(causal, ws_ref[grp], 0.0).astype(BF16)
        bias = bst_ref[:, grp:grp + 1]
        for c in range(tm // CHUNK):
            rows = slice(c * CHUNK, (c + 1) * CHUNK)
            s = _dot(w, vn_ref[rows, cols]) + bias
            gated_ref[rows, cols] = (u_all[rows, cols] * s).astype(BF16)

    o_ref[...] = x + _dot(gated_ref[...], wout_ref[...])


def _gmlp_layer(x, g, w_in, ln_g, ln_b, w_s, b_s, w_out, layer):
    t, d = x.shape
    e = ln_g.shape[0]
    tm = GMLP_ROWS
    return pl.pallas_call(
        _gmlp_kernel,
        grid=(t // tm,),
        in_specs=[
            pl.BlockSpec((tm, d), lambda i: (i, 0)),
            _resident((1, d)),
            _layer_resident(w_in.shape, layer),
            _resident((1, e)),
            _resident((1, e)),
            _layer_resident(w_s.shape, layer),
            _resident((CHUNK, GMLP_GROUPS)),
            _layer_resident(w_out.shape, layer),
        ],
        out_specs=pl.BlockSpec((tm, d), lambda i: (i, 0)),
        out_shape=jax.ShapeDtypeStruct((t, d), F32),
        scratch_shapes=[pltpu.VMEM((tm, e), BF16), pltpu.VMEM((tm, e), BF16)],
        compiler_params=_compiler_params(1),
        name="gmlp",
    )(x, g.reshape(1, d), w_in, ln_g.reshape(1, e), ln_b.reshape(1, e), w_s, b_s.T, w_out)


def _split3(x):
    hi = x.astype(BF16)
    r = x - hi.astype(F32)
    mid = r.astype(BF16)
    lo = (r - mid.astype(F32)).astype(BF16)
    return hi, mid, lo


def _store_pairs(dst_ref, col0, val):
    for c in range(0, val.shape[1], LANES):
        dst_ref[(col0 + c) // LANES] = val[:, c:c + LANES]


def _fox_proj_kernel(x_ref, g_ref, w_ref, wf_ref, bf_ref, qg_ref, kg_ref, gmat_ref,
                     eqk_ref, oneq_ref, onek_ref,
                     q_ref, k_ref, v_ref, gate_ref, qa_ref, ka_ref, carry_ref,
                     *, blocks_per_seq):
    tm = x_ref.shape[0]
    wdt = qg_ref.shape[1]
    i = pl.program_id(0)
    h = _rms_norm(x_ref[...], g_ref[...]).astype(BF16)

    gmat = gmat_ref[...]

    def normed_heads(base, gain_ref, dst):
        for s0 in range(0, wdt, PROJ_SLAB):
            y = _dot(h, w_ref[:, base + s0: base + s0 + PROJ_SLAB])
            for c0 in range(0, PROJ_SLAB, MXU_DIM):
                cols = slice(s0 + c0, s0 + c0 + MXU_DIM)
                yy = y[:, c0:c0 + MXU_DIM]
                ms = _dot((yy * yy).astype(BF16), gmat)
                _store_pairs(dst, s0 + c0, (yy * lax.rsqrt(ms + RMS_EPS) * gain_ref[:, cols]).astype(BF16))

    f = _dot(h, wf_ref[...]) + bf_ref[...]
    logf = jnp.minimum(f, 0.0) - jnp.log(1.0 + jnp.exp(-jnp.abs(f)))
    normed_heads(0, qg_ref, q_ref)

    row = lax.broadcasted_iota(jnp.int32, (tm, tm), 0)
    col = lax.broadcasted_iota(jnp.int32, (tm, tm), 1)
    tril = jnp.where(row >= col, 1.0, 0.0).astype(BF16)
    first = (i % blocks_per_seq) == 0
    carry = jnp.where(first, 0.0, carry_ref[0:1, :])
    csum = _dot(tril, jnp.concatenate(_split3(logf), axis=1))
    c = (csum[:, :LANES] + csum[:, LANES:2 * LANES]) + csum[:, 2 * LANES:] + carry
    carry_ref[0:1, :] = c[tm - 1:tm, :]
    normed_heads(wdt, kg_ref, k_ref)

    pieces = jnp.concatenate(_split3(c * LOG2E), axis=1)
    slabs = _dot(pieces, eqk_ref[...])
    qa_ref[...] = (slabs[:, :LANES] + oneq_ref[...]).astype(BF16)
    ka_ref[...] = (onek_ref[...] - slabs[:, LANES:]).astype(BF16)

    for s0 in range(0, wdt, PROJ_SLAB):
        _store_pairs(v_ref, s0, _dot(h, w_ref[:, 2 * wdt + s0: 2 * wdt + s0 + PROJ_SLAB]).astype(BF16))
        gt = _dot(h, w_ref[:, 3 * wdt + s0: 3 * wdt + s0 + PROJ_SLAB])
        _store_pairs(gate_ref, s0, (1.0 / (1.0 + jnp.exp(-gt))).astype(BF16))


def _bias_slab_constants(n_heads):
    eqk = np.zeros((3 * LANES, 2 * LANES), np.float32)
    oneq = np.zeros((1, LANES), np.float32)
    onek = np.zeros((1, LANES), np.float32)
    for h in range(n_heads):
        for p in range(3):
            eqk[p * LANES + h, AUG_PER_HEAD * h + p] = 1.0
            eqk[p * LANES + h, LANES + AUG_PER_HEAD * h + 3 + p] = 1.0
            oneq[0, AUG_PER_HEAD * h + 3 + p] = 1.0
            onek[0, AUG_PER_HEAD * h + p] = 1.0
    onek[0, AUG_PER_HEAD * n_heads: AUG_PER_HEAD * n_heads + SHIFT_LANES] = 1.0
    return jnp.asarray(eqk, BF16), jnp.asarray(oneq), jnp.asarray(onek)


def _head_mean_matrix():
    idx = np.arange(MXU_DIM) // HEAD_DIM
    return jnp.asarray((idx[:, None] == idx[None, :]).astype(np.float32) / HEAD_DIM, BF16)


def _fox_proj(x, g, w_in, layer, w_f, b_f, q_gain, k_gain, seq, logit_shift):
    t, d = x.shape
    wdt = q_gain.shape[1]
    n_heads = wdt // HEAD_DIM
    tm = PROJ_ROWS
    eqk, oneq, onek = _bias_slab_constants(n_heads)
    shift_pieces = jnp.stack([p.astype(F32) for p in _split3(-logit_shift)])
    oneq = lax.dynamic_update_slice(oneq, shift_pieces.reshape(1, SHIFT_LANES), (0, AUG_PER_HEAD * n_heads))
    gmat = _head_mean_matrix()
    row_spec = lambda width: pl.BlockSpec((tm, width), lambda i: (i, 0))
    bf = lambda width: jax.ShapeDtypeStruct((t, width), BF16)
    pairs = wdt // LANES
    pair_spec = pl.BlockSpec((pairs, tm, LANES), lambda i: (0, i, 0))
    pair_shape = jax.ShapeDtypeStruct((pairs, t, LANES), BF16)
    return pl.pallas_call(
        functools.partial(_fox_proj_kernel, blocks_per_seq=seq // tm),
        grid=(t // tm,),
        in_specs=[
            row_spec(d), _resident((1, d)), _layer_resident(w_in.shape, layer, width=4 * wdt),
            _resident(w_f.shape),
            _resident((1, LANES)), _resident((1, wdt)), _resident((1, wdt)),
            _resident(gmat.shape), _resident(eqk.shape),
            _resident((1, LANES)), _resident((1, LANES)),
        ],
        out_specs=[pair_spec] * 4 + [row_spec(LANES), row_spec(LANES)],
        out_shape=[pair_shape] * 4 + [bf(LANES), bf(LANES)],
        scratch_shapes=[pltpu.VMEM((8, LANES), F32)],
        compiler_params=_compiler_params(1),
        name="fox_proj",
    )(x, g.reshape(1, d), w_in, w_f, b_f, q_gain, k_gain, gmat, eqk, oneq, onek)


def _nt_dot(a, b):
    return lax.dot_general(a, b, (((1,), (1,)), ((), ())), preferred_element_type=F32)


def _fox_attn_kernel(q_ref, qa_ref, k_ref, ka_ref, v_ref, gate_ref, o_ref, kcat_ref, vt_ref,
                     *, use_row_max, shift_lo):
    seq = k_ref.shape[1]
    tq = ATT_Q
    group = pl.program_id(1)
    lane = lax.broadcasted_iota(jnp.int32, (1, LANES), 1)

    key_idx = lax.broadcasted_iota(jnp.int32, (tq, 2 * tq), 0)
    qry_idx = lax.broadcasted_iota(jnp.int32, (tq, 2 * tq), 1)
    visible = key_idx <= qry_idx - jnp.where(qry_idx >= tq, tq, 0)

    head_masks = []
    for pp in range(ATT_PAIRS):
        kcat_ref[pp, :, :LANES] = k_ref[pp]
        kcat_ref[pp, :, LANES:] = ka_ref[...]
        vt_ref[pp, :LANES, :] = v_ref[pp].astype(F32).T.astype(BF16)
        vt_ref[pp, LANES:, :] = jnp.ones((ONES_ROWS, seq), BF16)
        masks = []
        for e in range(2):
            in_half = (lane >= HEAD_DIM * e) & (lane < HEAD_DIM * (e + 1))
            aug_lo = AUG_PER_HEAD * (2 * (ATT_PAIRS * group + pp) + e)
            in_aug = ((lane >= aug_lo) & (lane < aug_lo + AUG_PER_HEAD)) | (
                (lane >= shift_lo) & (lane < shift_lo + SHIFT_LANES))
            masks.append((jnp.where(in_half, 1.0, 0.0).astype(BF16),
                          jnp.where(in_aug, 1.0, 0.0).astype(BF16)))
        head_masks.append(masks)

    def scores(pp, i):
        r0, r1 = i * tq, (i + 1) * tq
        q = q_ref[pp, r0:r1, :]
        qa = qa_ref[r0:r1, :]
        qboth = jnp.concatenate(
            [jnp.concatenate([q * half, qa * aug], axis=1) for half, aug in head_masks[pp]], axis=0)
        s_diag = jnp.where(visible, _nt_dot(kcat_ref[pp, r0:r1, :], qboth), NEG_BIG)
        s_past = _nt_dot(kcat_ref[pp, :r0, :], qboth) if i > 0 else None
        if use_row_max:
            m = jnp.max(s_diag, axis=0, keepdims=True)
            if i > 0:
                m = jnp.maximum(m, jnp.max(s_past, axis=0, keepdims=True))
                s_past = s_past - m
            s_diag = s_diag - m
        p_past = jnp.exp2(s_past).astype(BF16) if i > 0 else None
        return p_past, jnp.exp2(s_diag).astype(BF16)

    def finish(pp, i, p_past, p_diag):
        r0, r1 = i * tq, (i + 1) * tq
        acc = _dot(vt_ref[pp, :, r0:r1], p_diag)
        if i > 0:
            acc = acc + _dot(vt_ref[pp, :, :r0], p_past)
        inv = 1.0 / acc[LANES:LANES + 1, :]
        o = jnp.concatenate([acc[:HEAD_DIM, :tq] * inv[:, :tq],
                             acc[HEAD_DIM:LANES, tq:] * inv[:, tq:]], axis=0).T
        o_ref[pp, r0:r1, :] = (o * gate_ref[pp, r0:r1, :].astype(F32)).astype(BF16)

    order = [(pp, i) for pp in range(ATT_PAIRS) for i in range(seq // tq - 1, -1, -1)]
    pending = [scores(*blk) for blk in order[:ATT_AHEAD]]
    for n, blk in enumerate(order):
        if n + ATT_AHEAD < len(order):
            pending.append(scores(*order[n + ATT_AHEAD]))
        finish(*blk, *pending.pop(0))


def _fox_attn(q, k, v, gate, qa, ka, batch, seq, use_row_max):
    pairs, t, _ = q.shape
    pair_spec = pl.BlockSpec((ATT_PAIRS, seq, LANES), lambda b, p: (p, b, 0))
    aug_spec = pl.BlockSpec((seq, LANES), lambda b, p: (b, 0))
    return pl.pallas_call(
        functools.partial(_fox_attn_kernel, use_row_max=use_row_max, shift_lo=2 * pairs * AUG_PER_HEAD),
        grid=(batch, pairs // ATT_PAIRS),
        in_specs=[pair_spec, aug_spec, pair_spec, aug_spec, pair_spec, pair_spec],
        out_specs=pair_spec,
        out_shape=jax.ShapeDtypeStruct((pairs, t, LANES), BF16),
        scratch_shapes=[pltpu.VMEM((ATT_PAIRS, seq, 2 * LANES), BF16),
                        pltpu.VMEM((ATT_PAIRS, LANES + ONES_ROWS, seq), BF16)],
        compiler_params=_compiler_params(2),
        name="fox_attn_rowmax" if use_row_max else "fox_attn",
    )(q, qa, k, ka, v, gate)


def kernel(x, gmlp_w_in, gmlp_ln_g, gmlp_ln_b, gmlp_w_s, gmlp_b_s, gmlp_w_out, fox_w_in, fox_b_f, fox_q_g, fox_k_g, fox_w_out, mix_norm_g, mlp_norm_g, mlp_w1, mlp_w2):
    batch, seq, d = x.shape
    depth = mix_norm_g.shape[0]
    wdt = fox_w_out.shape[1]
    n_heads = wdt // HEAD_DIM
    xs = x.reshape(batch * seq, d)

    gmlp_w_in, gmlp_w_out, fox_w_in_bf, fox_w_out, mlp_w1, mlp_w2 = (
        w.astype(BF16) for w in (gmlp_w_in, gmlp_w_out, fox_w_in, fox_w_out, mlp_w1, mlp_w2))

    for layer in range(depth):
        j = layer // 2
        if layer % 2 == 0:
            xs = _gmlp_layer(xs, mix_norm_g[layer], gmlp_w_in, gmlp_ln_g[j], gmlp_ln_b[j],
                             gmlp_w_s, gmlp_b_s[j], gmlp_w_out, j)
            xs = _mlp_layer(xs, mlp_norm_g[layer], mlp_w1, mlp_w2, layer)
        else:
            w_f = jnp.pad(fox_w_in[j, :, 4 * wdt:], ((0, 0), (0, LANES - n_heads))).astype(BF16)
            b_f = jnp.pad(fox_b_f[j], (0, LANES - n_heads)).reshape(1, LANES)
            q_gain = (jnp.tile(fox_q_g[j], n_heads) * (HEAD_DIM ** -0.5 * LOG2E)).reshape(1, wdt)
            k_gain = jnp.tile(fox_k_g[j], n_heads).reshape(1, wdt)
            bound = (jnp.max(jnp.abs(fox_q_g[j])) * jnp.max(jnp.abs(fox_k_g[j]))) * (HEAD_DIM * HEAD_DIM ** -0.5 * LOG2E)
            fixed_shift_ok = bound <= ATT_MAX_FIXED_SHIFT
            q, k, v, gate, qa, ka = _fox_proj(xs, mix_norm_g[layer], fox_w_in_bf, j, w_f, b_f,
                                              q_gain, k_gain, seq, jnp.where(fixed_shift_ok, bound, 0.0))
            og = lax.cond(fixed_shift_ok,
                          functools.partial(_fox_attn, batch=batch, seq=seq, use_row_max=False),
                          functools.partial(_fox_attn, batch=batch, seq=seq, use_row_max=True),
                          q, k, v, gate, qa, ka)
            xs = _mlp_layer(xs, mlp_norm_g[layer], mlp_w1, mlp_w2, layer,
                            og=og, wo=fox_w_out, wo_layer=j)
    return xs.reshape(batch, seq, d)
```

```python
import functools
import math

import numpy as np
import jax
import jax.numpy as jnp
from jax import lax
from jax.experimental import pallas as pl
from jax.experimental.pallas import tpu as pltpu

F32 = jnp.float32
BF16 = jnp.bfloat16

CHUNK = 128
GMLP_GROUPS = 8
HEAD_DIM = 64
RMS_EPS = 1e-6
LN_EPS = 1e-5
LOG2E = math.log2(math.e)

LANES = 128
MXU_DIM = 256
VMEM_LIMIT_BYTES = 56 * 1024 * 1024

MLP_ROWS = 1024
MLP_FF_TILE = 512
GMLP_ROWS = 512
PROJ_ROWS = 512
PROJ_SLAB = 512
ATT_Q = 256
ATT_PAIRS = 4
ATT_AHEAD = 3
ONES_ROWS = 16
AUG_PER_HEAD = 6
SHIFT_LANES = 3
ATT_MAX_FIXED_SHIFT = 48.0
NEG_BIG = -1e30


def _compiler_params(n_axes):
    return pltpu.CompilerParams(
        dimension_semantics=("arbitrary",) * n_axes,
        vmem_limit_bytes=VMEM_LIMIT_BYTES,
    )


def _resident(shape):
    nd = len(shape)
    return pl.BlockSpec(shape, lambda *_: (0,) * nd, pipeline_mode=pl.Buffered(1))


def _layer_resident(stacked_shape, layer, width=None):
    tail = tuple(stacked_shape[1:-1]) + (width or stacked_shape[-1],)
    return pl.BlockSpec((None,) + tail, lambda *_: (layer,) + (0,) * len(tail),
                        pipeline_mode=pl.Buffered(1))


def _rms_norm(x, g):
    ms = jnp.mean(x * x, axis=-1, keepdims=True)
    return x * lax.rsqrt(ms + RMS_EPS) * g


def _gelu_tanh(x):
    c = math.sqrt(2.0 / math.pi)
    return x * (0.5 * (1.0 + jnp.tanh(c * (x + 0.044715 * (x * x * x)))))


def _dot(a, b):
    return jnp.dot(a, b, preferred_element_type=F32)


def _mlp_kernel(*refs, fused_out_proj):
    if fused_out_proj:
        x_ref, og_ref, wo_ref, g_ref, w1_ref, w2_ref, o_ref = refs
    else:
        x_ref, g_ref, w1_ref, w2_ref, o_ref = refs
    tf = MLP_FF_TILE
    x = x_ref[...]
    if fused_out_proj:
        og = jnp.concatenate([og_ref[p] for p in range(og_ref.shape[0])], axis=1)
        x = x + _dot(og, wo_ref[...])
    h = _rms_norm(x, g_ref[...]).astype(BF16)
    acc = x
    for f0 in range(0, w1_ref.shape[1], tf):
        a = jnp.maximum(_dot(h, w1_ref[:, f0:f0 + tf]), 0.0)
        acc = acc + _dot((a * a).astype(BF16), w2_ref[f0:f0 + tf, :])
    o_ref[...] = acc


def _mlp_layer(x, g, w1, w2, layer, og=None, wo=None, wo_layer=None):
    t, d = x.shape
    tm = MLP_ROWS
    fused = og is not None
    row_spec = lambda width: pl.BlockSpec((tm, width), lambda i: (i, 0))
    in_specs = [row_spec(d)]
    args = [x]
    if fused:
        in_specs += [pl.BlockSpec((og.shape[0], tm, LANES), lambda i: (0, i, 0)),
                     _layer_resident(wo.shape, wo_layer)]
        args += [og, wo]
    in_specs += [_resident((1, d)), _layer_resident(w1.shape, layer), _layer_resident(w2.shape, layer)]
    args += [g.reshape(1, d), w1, w2]
    return pl.pallas_call(
        functools.partial(_mlp_kernel, fused_out_proj=fused),
        grid=(t // tm,),
        in_specs=in_specs,
        out_specs=row_spec(d),
        out_shape=jax.ShapeDtypeStruct((t, d), F32),
        compiler_params=_compiler_params(1),
        name="mlp_fused" if fused else "mlp",
    )(*args)


def _mlp_stream_kernel(x_ref, g_ref, w1f_ref, w2f_ref, o_ref, w1_ref, w2_ref, h_ref, *, n_chunks):
    s = pl.program_id(0)

    @pl.when(s < n_chunks)
    def _():
        w1c = w1f_ref[...].astype(BF16)
        w2c = w2f_ref[...].astype(BF16)
        w1_ref[s] = w1c
        w2_ref[s] = w2c

        @pl.when(s == 0)
        def _():
            x = x_ref[...]
            h_ref[...] = _rms_norm(x, g_ref[...]).astype(BF16)
            o_ref[...] = x

        a = jnp.maximum(_dot(h_ref[...], w1c), 0.0)
        o_ref[...] += _dot((a * a).astype(BF16), w2c)

    @pl.when(s >= n_chunks)
    def _():
        x = x_ref[...]
        h = _rms_norm(x, g_ref[...]).astype(BF16)
        acc = x
        for j in range(n_chunks):
            a = jnp.maximum(_dot(h, w1_ref[j]), 0.0)
            acc = acc + _dot((a * a).astype(BF16), w2_ref[j])
        o_ref[...] = acc


def _mlp_layer_f32_weights(x, g, w1, w2, layer):
    t, d = x.shape
    ff = w1.shape[2]
    tm, tf = MLP_ROWS, MLP_FF_TILE
    n_chunks = ff // tf
    last = n_chunks - 1
    row_spec = pl.BlockSpec((tm, d), lambda s: (jnp.maximum(s - last, 0), 0))
    return pl.pallas_call(
        functools.partial(_mlp_stream_kernel, n_chunks=n_chunks),
        grid=(n_chunks + t // tm - 1,),
        in_specs=[
            row_spec,
            _resident((1, d)),
            pl.BlockSpec((None, d, tf), lambda s: (layer, 0, jnp.minimum(s, last))),
            pl.BlockSpec((None, tf, d), lambda s: (layer, jnp.minimum(s, last), 0)),
        ],
        out_specs=row_spec,
        out_shape=jax.ShapeDtypeStruct((t, d), F32),
        scratch_shapes=[pltpu.VMEM((n_chunks, d, tf), BF16), pltpu.VMEM((n_chunks, tf, d), BF16),
                        pltpu.VMEM((tm, d), BF16)],
        compiler_params=_compiler_params(1),
        name="mlp_f32w",
    )(x, g.reshape(1, d), w1, w2)


def _gmlp_kernel(x_ref, g_ref, win_ref, lng_ref, lnb_ref, ws_ref, bst_ref, wout_ref,
                 o_ref, vn_ref, gated_ref):
    tm = x_ref.shape[0]
    e = lng_ref.shape[1]
    gd = e // GMLP_GROUPS
    x = x_ref[...]
    h = _rms_norm(x, g_ref[...]).astype(BF16)

    v = _gelu_tanh(_dot(h, win_ref[:, e:]))
    u_all = _gelu_tanh(_dot(h, win_ref[:, :e]))
    mu = jnp.mean(v, axis=-1, keepdims=True)
    var = jnp.mean(v * v, axis=-1, keepdims=True) - mu * mu
    vn_ref[...] = ((v - mu) * lax.rsqrt(var + LN_EPS) * lng_ref[...] + lnb_ref[...]).astype(BF16)

    row = lax.broadcasted_iota(jnp.int32, (CHUNK, CHUNK), 0)
    col = lax.broadcasted_iota(jnp.int32, (CHUNK, CHUNK), 1)
    causal = row >= col
    for grp in range(GMLP_GROUPS):
        cols = slice(grp * gd, (grp + 1) * gd)
        w = jnp.where(causal, ws_ref[grp], 0.0).astype(BF16)
        bias = bst_ref[:, grp:grp + 1]
        for c in range(tm // CHUNK):
            rows = slice(c * CHUNK, (c + 1) * CHUNK)
            s = _dot(w, vn_ref[rows, cols]) + bias
            gated_ref[rows, cols] = (u_all[rows, cols] * s).astype(BF16)

    o_ref[...] = x + _dot(gated_ref[...], wout_ref[...])


def _gmlp_layer(x, g, w_in, ln_g, ln_b, w_s, b_s, w_out, layer):
    t, d = x.shape
    e = ln_g.shape[0]
    tm = GMLP_ROWS
    return pl.pallas_call(
        _gmlp_kernel,
        grid=(t // tm,),
        in_specs=[
            pl.BlockSpec((tm, d), lambda i: (i, 0)),
            _resident((1, d)),
            _layer_resident(w_in.shape, layer),
            _resident((1, e)),
            _resident((1, e)),
            _layer_resident(w_s.shape, layer),
            _resident((CHUNK, GMLP_GROUPS)),
            _layer_resident(w_out.shape, layer),
        ],
        out_specs=pl.BlockSpec((tm, d), lambda i: (i, 0)),
        out_shape=jax.ShapeDtypeStruct((t, d), F32),
        scratch_shapes=[pltpu.VMEM((tm, e), BF16), pltpu.VMEM((tm, e), BF16)],
        compiler_params=_compiler_params(1),
        name="gmlp",
    )(x, g.reshape(1, d), w_in, ln_g.reshape(1, e), ln_b.reshape(1, e), w_s, b_s.T, w_out)


def _split3(x):
    hi = x.astype(BF16)
    r = x - hi.astype(F32)
    mid = r.astype(BF16)
    lo = (r - mid.astype(F32)).astype(BF16)
    return hi, mid, lo


def _store_pairs(dst_ref, col0, val):
    for c in range(0, val.shape[1], LANES):
        dst_ref[(col0 + c) // LANES] = val[:, c:c + LANES]


def _fox_proj_kernel(x_ref, g_ref, w_ref, wf_ref, bf_ref, qg_ref, kg_ref, gmat_ref,
                     eqk_ref, oneq_ref, onek_ref,
                     q_ref, k_ref, v_ref, gate_ref, qa_ref, ka_ref, carry_ref,
                     *, blocks_per_seq):
    tm = x_ref.shape[0]
    wdt = qg_ref.shape[1]
    i = pl.program_id(0)
    h = _rms_norm(x_ref[...], g_ref[...]).astype(BF16)

    gmat = gmat_ref[...]

    def normed_heads(base, gain_ref, dst):
        for s0 in range(0, wdt, PROJ_SLAB):
            y = _dot(h, w_ref[:, base + s0: base + s0 + PROJ_SLAB])
            for c0 in range(0, PROJ_SLAB, MXU_DIM):
                cols = slice(s0 + c0, s0 + c0 + MXU_DIM)
                yy = y[:, c0:c0 + MXU_DIM]
                ms = _dot((yy * yy).astype(BF16), gmat)
                _store_pairs(dst, s0 + c0, (yy * lax.rsqrt(ms + RMS_EPS) * gain_ref[:, cols]).astype(BF16))

    f = _dot(h, wf_ref[...]) + bf_ref[...]
    logf = jnp.minimum(f, 0.0) - jnp.log(1.0 + jnp.exp(-jnp.abs(f)))
    normed_heads(0, qg_ref, q_ref)

    row = lax.broadcasted_iota(jnp.int32, (tm, tm), 0)
    col = lax.broadcasted_iota(jnp.int32, (tm, tm), 1)
    tril = jnp.where(row >= col, 1.0, 0.0).astype(BF16)
    first = (i % blocks_per_seq) == 0
    carry = jnp.where(first, 0.0, carry_ref[0:1, :])
    csum = _dot(tril, jnp.concatenate(_split3(logf), axis=1))
    c = (csum[:, :LANES] + csum[:, LANES:2 * LANES]) + csum[:, 2 * LANES:] + carry
    carry_ref[0:1, :] = c[tm - 1:tm, :]
    normed_heads(wdt, kg_ref, k_ref)

    pieces = jnp.concatenate(_split3(c * LOG2E), axis=1)
    slabs = _dot(pieces, eqk_ref[...])
    qa_ref[...] = (slabs[:, :LANES] + oneq_ref[...]).astype(BF16)
    ka_ref[...] = (onek_ref[...] - slabs[:, LANES:]).astype(BF16)

    for s0 in range(0, wdt, PROJ_SLAB):
        _store_pairs(v_ref, s0, _dot(h, w_ref[:, 2 * wdt + s0: 2 * wdt + s0 + PROJ_SLAB]).astype(BF16))
        gt = _dot(h, w_ref[:, 3 * wdt + s0: 3 * wdt + s0 + PROJ_SLAB])
        _store_pairs(gate_ref, s0, (1.0 / (1.0 + jnp.exp(-gt))).astype(BF16))


def _bias_slab_constants(n_heads):
    eqk = np.zeros((3 * LANES, 2 * LANES), np.float32)
    oneq = np.zeros((1, LANES), np.float32)
    onek = np.zeros((1, LANES), np.float32)
    for h in range(n_heads):
        for p in range(3):
            eqk[p * LANES + h, AUG_PER_HEAD * h + p] = 1.0
            eqk[p * LANES + h, LANES + AUG_PER_HEAD * h + 3 + p] = 1.0
            oneq[0, AUG_PER_HEAD * h + 3 + p] = 1.0
            onek[0, AUG_PER_HEAD * h + p] = 1.0
    onek[0, AUG_PER_HEAD * n_heads: AUG_PER_HEAD * n_heads + SHIFT_LANES] = 1.0
    return jnp.asarray(eqk, BF16), jnp.asarray(oneq), jnp.asarray(onek)


def _head_mean_matrix():
    idx = np.arange(MXU_DIM) // HEAD_DIM
    return jnp.asarray((idx[:, None] == idx[None, :]).astype(np.float32) / HEAD_DIM, BF16)


def _fox_proj(x, g, w_in, layer, w_f, b_f, q_gain, k_gain, seq, logit_shift):
    t, d = x.shape
    wdt = q_gain.shape[1]
    n_heads = wdt // HEAD_DIM
    tm = PROJ_ROWS
    eqk, oneq, onek = _bias_slab_constants(n_heads)
    shift_pieces = jnp.stack([p.astype(F32) for p in _split3(-logit_shift)])
    oneq = lax.dynamic_update_slice(oneq, shift_pieces.reshape(1, SHIFT_LANES), (0, AUG_PER_HEAD * n_heads))
    gmat = _head_mean_matrix()
    row_spec = lambda width: pl.BlockSpec((tm, width), lambda i: (i, 0))
    bf = lambda width: jax.ShapeDtypeStruct((t, width), BF16)
    pairs = wdt // LANES
    pair_spec = pl.BlockSpec((pairs, tm, LANES), lambda i: (0, i, 0))
    pair_shape = jax.ShapeDtypeStruct((pairs, t, LANES), BF16)
    return pl.pallas_call(
        functools.partial(_fox_proj_kernel, blocks_per_seq=seq // tm),
        grid=(t // tm,),
        in_specs=[
            row_spec(d), _resident((1, d)), _layer_resident(w_in.shape, layer, width=4 * wdt),
            _resident(w_f.shape),
            _resident((1, LANES)), _resident((1, wdt)), _resident((1, wdt)),
            _resident(gmat.shape), _resident(eqk.shape),
            _resident((1, LANES)), _resident((1, LANES)),
        ],
        out_specs=[pair_spec] * 4 + [row_spec(LANES), row_spec(LANES)],
        out_shape=[pair_shape] * 4 + [bf(LANES), bf(LANES)],
        scratch_shapes=[pltpu.VMEM((8, LANES), F32)],
        compiler_params=_compiler_params(1),
        name="fox_proj",
    )(x, g.reshape(1, d), w_in, w_f, b_f, q_gain, k_gain, gmat, eqk, oneq, onek)


def _nt_dot(a, b):
    return lax.dot_general(a, b, (((1,), (1,)), ((), ())), preferred_element_type=F32)


def _fox_attn_kernel(q_ref, qa_ref, k_ref, ka_ref, v_ref, gate_ref, o_ref, kcat_ref, vt_ref,
                     *, use_row_max, shift_lo):
    seq = k_ref.shape[1]
    tq = ATT_Q
    group = pl.program_id(1)
    lane = lax.broadcasted_iota(jnp.int32, (1, LANES), 1)

    key_idx = lax.broadcasted_iota(jnp.int32, (tq, 2 * tq), 0)
    qry_idx = lax.broadcasted_iota(jnp.int32, (tq, 2 * tq), 1)
    visible = key_idx <= qry_idx - jnp.where(qry_idx >= tq, tq, 0)

    head_masks = []
    for pp in range(ATT_PAIRS):
        kcat_ref[pp, :, :LANES] = k_ref[pp]
        kcat_ref[pp, :, LANES:] = ka_ref[...]
        vt_ref[pp, :LANES, :] = v_ref[pp].astype(F32).T.astype(BF16)
        vt_ref[pp, LANES:, :] = jnp.ones((ONES_ROWS, seq), BF16)
        masks = []
        for e in range(2):
            in_half = (lane >= HEAD_DIM * e) & (lane < HEAD_DIM * (e + 1))
            aug_lo = AUG_PER_HEAD * (2 * (ATT_PAIRS * group + pp) + e)
            in_aug = ((lane >= aug_lo) & (lane < aug_lo + AUG_PER_HEAD)) | (
                (lane >= shift_lo) & (lane < shift_lo + SHIFT_LANES))
            masks.append((jnp.where(in_half, 1.0, 0.0).astype(BF16),
                          jnp.where(in_aug, 1.0, 0.0).astype(BF16)))
        head_masks.append(masks)

    def scores(pp, i):
        r0, r1 = i * tq, (i + 1) * tq
        q = q_ref[pp, r0:r1, :]
        qa = qa_ref[r0:r1, :]
        qboth = jnp.concatenate(
            [jnp.concatenate([q * half, qa * aug], axis=1) for half, aug in head_masks[pp]], axis=0)
        s_diag = jnp.where(visible, _nt_dot(kcat_ref[pp, r0:r1, :], qboth), NEG_BIG)
        s_past = _nt_dot(kcat_ref[pp, :r0, :], qboth) if i > 0 else None
        if use_row_max:
            m = jnp.max(s_diag, axis=0, keepdims=True)
            if i > 0:
                m = jnp.maximum(m, jnp.max(s_past, axis=0, keepdims=True))
                s_past = s_past - m
            s_diag = s_diag - m
        p_past = jnp.exp2(s_past).astype(BF16) if i > 0 else None
        return p_past, jnp.exp2(s_diag).astype(BF16)

    def finish(pp, i, p_past, p_diag):
        r0, r1 = i * tq, (i + 1) * tq
        acc = _dot(vt_ref[pp, :, r0:r1], p_diag)
        if i > 0:
            acc = acc + _dot(vt_ref[pp, :, :r0], p_past)
        inv = 1.0 / acc[LANES:LANES + 1, :]
        o = jnp.concatenate([acc[:HEAD_DIM, :tq] * inv[:, :tq],
                             acc[HEAD_DIM:LANES, tq:] * inv[:, tq:]], axis=0).T
        o_ref[pp, r0:r1, :] = (o * gate_ref[pp, r0:r1, :].astype(F32)).astype(BF16)

    order = [(pp, i) for pp in range(ATT_PAIRS) for i in range(seq // tq - 1, -1, -1)]
    pending = [scores(*blk) for blk in order[:ATT_AHEAD]]
    for n, blk in enumerate(order):
        if n + ATT_AHEAD < len(order):
            pending.append(scores(*order[n + ATT_AHEAD]))
        finish(*blk, *pending.pop(0))


def _fox_attn(q, k, v, gate, qa, ka, batch, seq, use_row_max):
    pairs, t, _ = q.shape
    pair_spec = pl.BlockSpec((ATT_PAIRS, seq, LANES), lambda b, p: (p, b, 0))
    aug_spec = pl.BlockSpec((seq, LANES), lambda b, p: (b, 0))
    return pl.pallas_call(
        functools.partial(_fox_attn_kernel, use_row_max=use_row_max, shift_lo=2 * pairs * AUG_PER_HEAD),
        grid=(batch, pairs // ATT_PAIRS),
        in_specs=[pair_spec, aug_spec, pair_spec, aug_spec, pair_spec, pair_spec],
        out_specs=pair_spec,
        out_shape=jax.ShapeDtypeStruct((pairs, t, LANES), BF16),
        scratch_shapes=[pltpu.VMEM((ATT_PAIRS, seq, 2 * LANES), BF16),
                        pltpu.VMEM((ATT_PAIRS, LANES + ONES_ROWS, seq), BF16)],
        compiler_params=_compiler_params(2),
        name="fox_attn_rowmax" if use_row_max else "fox_attn",
    )(q, qa, k, ka, v, gate)


def kernel(x, gmlp_w_in, gmlp_ln_g, gmlp_ln_b, gmlp_w_s, gmlp_b_s, gmlp_w_out, fox_w_in, fox_b_f, fox_q_g, fox_k_g, fox_w_out, mix_norm_g, mlp_norm_g, mlp_w1, mlp_w2):
    batch, seq, d = x.shape
    depth = mix_norm_g.shape[0]
    wdt = fox_w_out.shape[1]
    n_heads = wdt // HEAD_DIM
    xs = x.reshape(batch * seq, d)

    gmlp_w_in, gmlp_w_out, fox_w_in_bf, fox_w_out, mlp_w1_fox, mlp_w2_fox = (
        w.astype(BF16) for w in (gmlp_w_in, gmlp_w_out, fox_w_in, fox_w_out, mlp_w1[1::2], mlp_w2[1::2]))

    for layer in range(depth):
        j = layer // 2
        if layer % 2 == 0:
            xs = _gmlp_layer(xs, mix_norm_g[layer], gmlp_w_in, gmlp_ln_g[j], gmlp_ln_b[j],
                             gmlp_w_s, gmlp_b_s[j], gmlp_w_out, j)
            xs = _mlp_layer_f32_weights(xs, mlp_norm_g[layer], mlp_w1, mlp_w2, layer)
        else:
            w_f = jnp.pad(fox_w_in[j, :, 4 * wdt:], ((0, 0), (0, LANES - n_heads))).astype(BF16)
            b_f = jnp.pad(fox_b_f[j], (0, LANES - n_heads)).reshape(1, LANES)
            q_gain = (jnp.tile(fox_q_g[j], n_heads) * (HEAD_DIM ** -0.5 * LOG2E)).reshape(1, wdt)
            k_gain = jnp.tile(fox_k_g[j], n_heads).reshape(1, wdt)
            bound = (jnp.max(jnp.abs(fox_q_g[j])) * jnp.max(jnp.abs(fox_k_g[j]))) * (HEAD_DIM * HEAD_DIM ** -0.5 * LOG2E)
            fixed_shift_ok = bound <= ATT_MAX_FIXED_SHIFT
            q, k, v, gate, qa, ka = _fox_proj(xs, mix_norm_g[layer], fox_w_in_bf, j, w_f, b_f,
                                              q_gain, k_gain, seq, jnp.where(fixed_shift_ok, bound, 0.0))
            og = lax.cond(fixed_shift_ok,
                          functools.partial(_fox_attn, batch=batch, seq=seq, use_row_max=False),
                          functools.partial(_fox_attn, batch=batch, seq=seq, use_row_max=True),
                          q, k, v, gate, qa, ka)
            xs = _mlp_layer(xs, mlp_norm_g[layer], mlp_w1_fox, mlp_w2_fox, j,
                            og=og, wo=fox_w_out, wo_layer=j)
    return xs.reshape(batch, seq, d)
```

```python
import functools
import math

import numpy as np
import jax
import jax.numpy as jnp
from jax import lax
from jax.experimental import pallas as pl
from jax.experimental.pallas import tpu as pltpu

F32 = jnp.float32
BF16 = jnp.bfloat16

CHUNK = 128
GMLP_GROUPS = 8
HEAD_DIM = 64
RMS_EPS = 1e-6
LN_EPS = 1e-5
LOG2E = math.log2(math.e)

LANES = 128
MXU_DIM = 256
VMEM_LIMIT_BYTES = 56 * 1024 * 1024

MLP_ROWS = 1024
MLP_FF_TILE = 512
CONVERT_COLS = 512
GMLP_ROWS = 512
PROJ_ROWS = 512
PROJ_SLAB = 512
ATT_Q = 256
ATT_PAIRS = 2
ATT_AHEAD = 3
ONES_ROWS = 16
AUG_PER_HEAD = 6
SHIFT_LANES = 3
ATT_MAX_FIXED_SHIFT = 48.0
NEG_BIG = -1e30


def _compiler_params(n_axes):
    return pltpu.CompilerParams(
        dimension_semantics=("arbitrary",) * n_axes,
        vmem_limit_bytes=VMEM_LIMIT_BYTES,
    )


def _resident(shape):
    nd = len(shape)
    return pl.BlockSpec(shape, lambda *_: (0,) * nd, pipeline_mode=pl.Buffered(1))


def _layer_resident(stacked_shape, layer, width=None):
    tail = tuple(stacked_shape[1:-1]) + (width or stacked_shape[-1],)
    return pl.BlockSpec((None,) + tail, lambda *_: (layer,) + (0,) * len(tail),
                        pipeline_mode=pl.Buffered(1))


def _rms_norm(x, g):
    ms = jnp.mean(x * x, axis=-1, keepdims=True)
    return x * lax.rsqrt(ms + RMS_EPS) * g


def _gelu_tanh(x):
    c = math.sqrt(2.0 / math.pi)
    return x * (0.5 * (1.0 + jnp.tanh(c * (x + 0.044715 * (x * x * x)))))


def _dot(a, b):
    return jnp.dot(a, b, preferred_element_type=F32)


def _mlp_kernel(*refs, fused_out_proj):
    if fused_out_proj:
        x_ref, og_ref, wo_ref, g_ref, w1_ref, w2_ref, o_ref = refs
    else:
        x_ref, g_ref, w1_ref, w2_ref, o_ref = refs
    tf = MLP_FF_TILE
    x = x_ref[...]
    if fused_out_proj:
        og = jnp.concatenate([og_ref[p] for p in range(og_ref.shape[0])], axis=1)
        x = x + _dot(og, wo_ref[...])
    h = _rms_norm(x, g_ref[...]).astype(BF16)
    acc = x
    for f0 in range(0, w1_ref.shape[1], tf):
        a = jnp.maximum(_dot(h, w1_ref[:, f0:f0 + tf]), 0.0)
        acc = acc + _dot((a * a).astype(BF16), w2_ref[f0:f0 + tf, :])
    o_ref[...] = acc


def _mlp_layer(x, g, w1, w2, layer, og=None, wo=None, wo_layer=None):
    t, d = x.shape
    tm = MLP_ROWS
    fused = og is not None
    row_spec = lambda width: pl.BlockSpec((tm, width), lambda i: (i, 0))
    in_specs = [row_spec(d)]
    args = [x]
    if fused:
        in_specs += [pl.BlockSpec((og.shape[0], tm, LANES), lambda i: (0, i, 0)),
                     _layer_resident(wo.shape, wo_layer)]
        args += [og, wo]
    in_specs += [_resident((1, d)), _layer_resident(w1.shape, layer), _layer_resident(w2.shape, layer)]
    args += [g.reshape(1, d), w1, w2]
    return pl.pallas_call(
        functools.partial(_mlp_kernel, fused_out_proj=fused),
        grid=(t // tm,),
        in_specs=in_specs,
        out_specs=row_spec(d),
        out_shape=jax.ShapeDtypeStruct((t, d), F32),
        compiler_params=_compiler_params(1),
        name="mlp_fused" if fused else "mlp",
    )(*args)


def _mlp_stream_kernel(x_ref, g_ref, w1f_ref, w2f_ref, o_ref, w1_ref, w2_ref, h_ref, *, n_chunks):
    s = pl.program_id(0)

    @pl.when(s < n_chunks)
    def _():
        w1c = w1f_ref[...].astype(BF16)
        w2c = w2f_ref[...].astype(BF16)
        w1_ref[s] = w1c
        w2_ref[s] = w2c

        @pl.when(s == 0)
        def _():
            x = x_ref[...]
            h_ref[...] = _rms_norm(x, g_ref[...]).astype(BF16)
            o_ref[...] = x

        a = jnp.maximum(_dot(h_ref[...], w1c), 0.0)
        o_ref[...] += _dot((a * a).astype(BF16), w2c)

    @pl.when(s >= n_chunks)
    def _():
        x = x_ref[...]
        h = _rms_norm(x, g_ref[...]).astype(BF16)
        acc = x
        for j in range(n_chunks):
            a = jnp.maximum(_dot(h, w1_ref[j]), 0.0)
            acc = acc + _dot((a * a).astype(BF16), w2_ref[j])
        o_ref[...] = acc


def _convert_kernel(src_ref, dst_ref):
    dst_ref[...] = src_ref[...].astype(dst_ref.dtype)


def _to_bf16_layers(w, layers, col_block):
    _, rows, cols = w.shape
    first, step = layers[0], layers[1] - layers[0]
    assert list(layers) == [first + step * n for n in range(len(layers))]
    return pl.pallas_call(
        _convert_kernel,
        grid=(len(layers), cols // col_block),
        in_specs=[pl.BlockSpec((None, rows, col_block), lambda n, j: (first + step * n, 0, j))],
        out_specs=pl.BlockSpec((None, rows, col_block), lambda n, j: (n, 0, j)),
        out_shape=jax.ShapeDtypeStruct((len(layers), rows, cols), BF16),
        compiler_params=_compiler_params(2),
        name="to_bf16",
    )(w)


def _mlp_layer_f32_weights(x, g, w1, w2, layer):
    t, d = x.shape
    ff = w1.shape[2]
    tm, tf = MLP_ROWS, MLP_FF_TILE
    n_chunks = ff // tf
    last = n_chunks - 1
    row_spec = pl.BlockSpec((tm, d), lambda s: (jnp.maximum(s - last, 0), 0))
    return pl.pallas_call(
        functools.partial(_mlp_stream_kernel, n_chunks=n_chunks),
        grid=(n_chunks + t // tm - 1,),
        in_specs=[
            row_spec,
            _resident((1, d)),
            pl.BlockSpec((None, d, tf), lambda s: (layer, 0, jnp.minimum(s, last))),
            pl.BlockSpec((None, tf, d), lambda s: (layer, jnp.minimum(s, last), 0)),
        ],
        out_specs=row_spec,
        out_shape=jax.ShapeDtypeStruct((t, d), F32),
        scratch_shapes=[pltpu.VMEM((n_chunks, d, tf), BF16), pltpu.VMEM((n_chunks, tf, d), BF16),
                        pltpu.VMEM((tm, d), BF16)],
        compiler_params=_compiler_params(1),
        name="mlp_f32w",
    )(x, g.reshape(1, d), w1, w2)


def _gmlp_kernel(x_ref, g_ref, win_ref, lng_ref, lnb_ref, ws_ref, bst_ref, wout_ref,
                 o_ref, vn_ref, gated_ref):
    tm = x_ref.shape[0]
    e = lng_ref.shape[1]
    gd = e // GMLP_GROUPS
    x = x_ref[...]
    h = _rms_norm(x, g_ref[...]).astype(BF16)

    v = _gelu_tanh(_dot(h, win_ref[:, e:]))
    u_all = _gelu_tanh(_dot(h, win_ref[:, :e]))
    mu = jnp.mean(v, axis=-1, keepdims=True)
    var = jnp.mean(v * v, axis=-1, keepdims=True) - mu * mu
    vn_ref[...] = ((v - mu) * lax.rsqrt(var + LN_EPS) * lng_ref[...] + lnb_ref[...]).astype(BF16)

    row = lax.broadcasted_iota(jnp.int32, (CHUNK, CHUNK), 0)
    col = lax.broadcasted_iota(jnp.int32, (CHUNK, CHUNK), 1)
    causal = row >= col
    for grp in range(GMLP_GROUPS):
        cols = slice(grp * gd, (grp + 1) * gd)
        w = jnp.where(causal, ws_ref[grp], 0.0).astype(BF16)
        bias = bst_ref[:, grp:grp + 1]
        for c in range(tm // CHUNK):
            rows = slice(c * CHUNK, (c + 1) * CHUNK)
            s = _dot(w, vn_ref[rows, cols]) + bias
            gated_ref[rows, cols] = (u_all[rows, cols] * s).astype(BF16)

    o_ref[...] = x + _dot(gated_ref[...], wout_ref[...])


def _gmlp_layer(x, g, w_in, ln_g, ln_b, w_s, b_s, w_out, layer):
    t, d = x.shape
    e = ln_g.shape[0]
    tm = GMLP_ROWS
    return pl.pallas_call(
        _gmlp_kernel,
        grid=(t // tm,),
        in_specs=[
            pl.BlockSpec((tm, d), lambda i: (i, 0)),
            _resident((1, d)),
            _layer_resident(w_in.shape, layer),
            _resident((1, e)),
            _resident((1, e)),
            _layer_resident(w_s.shape, layer),
            _resident((CHUNK, GMLP_GROUPS)),
            _layer_resident(w_out.shape, layer),
        ],
        out_specs=pl.BlockSpec((tm, d), lambda i: (i, 0)),
        out_shape=jax.ShapeDtypeStruct((t, d), F32),
        scratch_shapes=[pltpu.VMEM((tm, e), BF16), pltpu.VMEM((tm, e), BF16)],
        compiler_params=_compiler_params(1),
        name="gmlp",
    )(x, g.reshape(1, d), w_in, ln_g.reshape(1, e), ln_b.reshape(1, e), w_s, b_s.T, w_out)


def _split3(x):
    hi = x.astype(BF16)
    r = x - hi.astype(F32)
    mid = r.astype(BF16)
    lo = (r - mid.astype(F32)).astype(BF16)
    return hi, mid, lo


def _store_pairs(dst_ref, col0, val):
    for c in range(0, val.shape[1], LANES):
        dst_ref[(col0 + c) // LANES] = val[:, c:c + LANES]


def _fox_proj_kernel(x_ref, g_ref, w_ref, wf_ref, bf_ref, qg_ref, kg_ref, gmat_ref,
                     eqk_ref, oneq_ref, onek_ref,
                     q_ref, k_ref, v_ref, gate_ref, qa_ref, ka_ref, carry_ref,
                     *, blocks_per_seq):
    tm = x_ref.shape[0]
    wdt = qg_ref.shape[1]
    i = pl.program_id(0)
    h = _rms_norm(x_ref[...], g_ref[...]).astype(BF16)

    gmat = gmat_ref[...]

    def normed_heads(base, gain_ref, dst):
        for s0 in range(0, wdt, PROJ_SLAB):
            y = _dot(h, w_ref[:, base + s0: base + s0 + PROJ_SLAB])
            for c0 in range(0, PROJ_SLAB, MXU_DIM):
                cols = slice(s0 + c0, s0 + c0 + MXU_DIM)
                yy = y[:, c0:c0 + MXU_DIM]
                ms = _dot((yy * yy).astype(BF16), gmat)
                _store_pairs(dst, s0 + c0, (yy * lax.rsqrt(ms + RMS_EPS) * gain_ref[:, cols]).astype(BF16))

    f = _dot(h, wf_ref[...]) + bf_ref[...]
    logf = jnp.minimum(f, 0.0) - jnp.log(1.0 + jnp.exp(-jnp.abs(f)))
    normed_heads(0, qg_ref, q_ref)

    row = lax.broadcasted_iota(jnp.int32, (tm, tm), 0)
    col = lax.broadcasted_iota(jnp.int32, (tm, tm), 1)
    tril = jnp.where(row >= col, 1.0, 0.0).astype(BF16)
    first = (i % blocks_per_seq) == 0
    carry = jnp.where(first, 0.0, carry_ref[0:1, :])
    csum = _dot(tril, jnp.concatenate(_split3(logf), axis=1))
    c = (csum[:, :LANES] + csum[:, LANES:2 * LANES]) + csum[:, 2 * LANES:] + carry
    carry_ref[0:1, :] = c[tm - 1:tm, :]
    normed_heads(wdt, kg_ref, k_ref)

    pieces = jnp.concatenate(_split3(c * LOG2E), axis=1)
    slabs = _dot(pieces, eqk_ref[...])
    qa_ref[...] = (slabs[:, :LANES] + oneq_ref[...]).astype(BF16)
    ka_ref[...] = (onek_ref[...] - slabs[:, LANES:]).astype(BF16)

    for s0 in range(0, wdt, PROJ_SLAB):
        _store_pairs(v_ref, s0, _dot(h, w_ref[:, 2 * wdt + s0: 2 * wdt + s0 + PROJ_SLAB]).astype(BF16))
        gt = _dot(h, w_ref[:, 3 * wdt + s0: 3 * wdt + s0 + PROJ_SLAB])
        _store_pairs(gate_ref, s0, (1.0 / (1.0 + jnp.exp(-gt))).astype(BF16))


def _bias_slab_constants(n_heads):
    eqk = np.zeros((3 * LANES, 2 * LANES), np.float32)
    oneq = np.zeros((1, LANES), np.float32)
    onek = np.zeros((1, LANES), np.float32)
    for h in range(n_heads):
        for p in range(3):
            eqk[p * LANES + h, AUG_PER_HEAD * h + p] = 1.0
            eqk[p * LANES + h, LANES + AUG_PER_HEAD * h + 3 + p] = 1.0
            oneq[0, AUG_PER_HEAD * h + 3 + p] = 1.0
            onek[0, AUG_PER_HEAD * h + p] = 1.0
    onek[0, AUG_PER_HEAD * n_heads: AUG_PER_HEAD * n_heads + SHIFT_LANES] = 1.0
    return jnp.asarray(eqk, BF16), jnp.asarray(oneq), jnp.asarray(onek)


def _head_mean_matrix():
    idx = np.arange(MXU_DIM) // HEAD_DIM
    return jnp.asarray((idx[:, None] == idx[None, :]).astype(np.float32) / HEAD_DIM, BF16)


def _fox_proj(x, g, w_in, layer, w_f, b_f, q_gain, k_gain, seq, logit_shift):
    t, d = x.shape
    wdt = q_gain.shape[1]
    n_heads = wdt // HEAD_DIM
    tm = PROJ_ROWS
    eqk, oneq, onek = _bias_slab_constants(n_heads)
    shift_pieces = jnp.stack([p.astype(F32) for p in _split3(-logit_shift)])
    oneq = lax.dynamic_update_slice(oneq, shift_pieces.reshape(1, SHIFT_LANES), (0, AUG_PER_HEAD * n_heads))
    gmat = _head_mean_matrix()
    row_spec = lambda width: pl.BlockSpec((tm, width), lambda i: (i, 0))
    bf = lambda width: jax.ShapeDtypeStruct((t, width), BF16)
    pairs = wdt // LANES
    pair_spec = pl.BlockSpec((pairs, tm, LANES), lambda i: (0, i, 0))
    pair_shape = jax.ShapeDtypeStruct((pairs, t, LANES), BF16)
    return pl.pallas_call(
        functools.partial(_fox_proj_kernel, blocks_per_seq=seq // tm),
        grid=(t // tm,),
        in_specs=[
            row_spec(d), _resident((1, d)), _layer_resident(w_in.shape, layer, width=4 * wdt),
            _resident(w_f.shape),
            _resident((1, LANES)), _resident((1, wdt)), _resident((1, wdt)),
            _resident(gmat.shape), _resident(eqk.shape),
            _resident((1, LANES)), _resident((1, LANES)),
        ],
        out_specs=[pair_spec] * 4 + [row_spec(LANES), row_spec(LANES)],
        out_shape=[pair_shape] * 4 + [bf(LANES), bf(LANES)],
        scratch_shapes=[pltpu.VMEM((8, LANES), F32)],
        compiler_params=_compiler_params(1),
        name="fox_proj",
    )(x, g.reshape(1, d), w_in, w_f, b_f, q_gain, k_gain, gmat, eqk, oneq, onek)


def _nt_dot(a, b):
    return lax.dot_general(a, b, (((1,), (1,)), ((), ())), preferred_element_type=F32)


def _fox_attn_kernel(q_ref, qa_ref, k_ref, ka_ref, v_ref, gate_ref, o_ref, kcat_ref, vt_ref,
                     *, use_row_max, shift_lo):
    seq = k_ref.shape[1]
    tq = ATT_Q
    group = pl.program_id(1)
    lane = lax.broadcasted_iota(jnp.int32, (1, LANES), 1)

    key_idx = lax.broadcasted_iota(jnp.int32, (tq, 2 * tq), 0)
    qry_idx = lax.broadcasted_iota(jnp.int32, (tq, 2 * tq), 1)
    visible = key_idx <= qry_idx - jnp.where(qry_idx >= tq, tq, 0)

    head_masks = []
    for pp in range(ATT_PAIRS):
        kcat_ref[pp, :, :LANES] = k_ref[pp]
        kcat_ref[pp, :, LANES:] = ka_ref[...]
        vt_ref[pp, :LANES, :] = v_ref[pp].astype(F32).T.astype(BF16)
        vt_ref[pp, LANES:, :] = jnp.ones((ONES_ROWS, seq), BF16)
        masks = []
        for e in range(2):
            in_half = (lane >= HEAD_DIM * e) & (lane < HEAD_DIM * (e + 1))
            aug_lo = AUG_PER_HEAD * (2 * (ATT_PAIRS * group + pp) + e)
            in_aug = ((lane >= aug_lo) & (lane < aug_lo + AUG_PER_HEAD)) | (
                (lane >= shift_lo) & (lane < shift_lo + SHIFT_LANES))
            masks.append((jnp.where(in_half, 1.0, 0.0).astype(BF16),
                          jnp.where(in_aug, 1.0, 0.0).astype(BF16)))
        head_masks.append(masks)

    def scores(pp, i):
        r0, r1 = i * tq, (i + 1) * tq
        q = q_ref[pp, r0:r1, :]
        qa = qa_ref[r0:r1, :]
        qboth = jnp.concatenate(
            [jnp.concatenate([q * half, qa * aug], axis=1) for half, aug in head_masks[pp]], axis=0)
        s_diag = jnp.where(visible, _nt_dot(kcat_ref[pp, r0:r1, :], qboth), NEG_BIG)
        s_past = _nt_dot(kcat_ref[pp, :r0, :], qboth) if i > 0 else None
        if use_row_max:
            m = jnp.max(s_diag, axis=0, keepdims=True)
            if i > 0:
                m = jnp.maximum(m, jnp.max(s_past, axis=0, keepdims=True))
                s_past = s_past - m
            s_diag = s_diag - m
        p_past = jnp.exp2(s_past).astype(BF16) if i > 0 else None
        return p_past, jnp.exp2(s_diag).astype(BF16)

    def finish(pp, i, p_past, p_diag):
        r0, r1 = i * tq, (i + 1) * tq
        acc = _dot(vt_ref[pp, :, r0:r1], p_diag)
        if i > 0:
            acc = acc + _dot(vt_ref[pp, :, :r0], p_past)
        inv = 1.0 / acc[LANES:LANES + 1, :]
        o = jnp.concatenate([acc[:HEAD_DIM, :tq] * inv[:, :tq],
                             acc[HEAD_DIM:LANES, tq:] * inv[:, tq:]], axis=0).T
        o_ref[pp, r0:r1, :] = (o * gate_ref[pp, r0:r1, :].astype(F32)).astype(BF16)

    order = [(pp, i) for pp in range(ATT_PAIRS) for i in range(seq // tq - 1, -1, -1)]
    pending = [scores(*blk) for blk in order[:ATT_AHEAD]]
    for n, blk in enumerate(order):
        if n + ATT_AHEAD < len(order):
            pending.append(scores(*order[n + ATT_AHEAD]))
        finish(*blk, *pending.pop(0))


def _fox_attn(q, k, v, gate, qa, ka, batch, seq, use_row_max):
    pairs, t, _ = q.shape
    pair_spec = pl.BlockSpec((ATT_PAIRS, seq, LANES), lambda b, p: (p, b, 0))
    aug_spec = pl.BlockSpec((seq, LANES), lambda b, p: (b, 0))
    return pl.pallas_call(
        functools.partial(_fox_attn_kernel, use_row_max=use_row_max, shift_lo=2 * pairs * AUG_PER_HEAD),
        grid=(batch, pairs // ATT_PAIRS),
        in_specs=[pair_spec, aug_spec, pair_spec, aug_spec, pair_spec, pair_spec],
        out_specs=pair_spec,
        out_shape=jax.ShapeDtypeStruct((pairs, t, LANES), BF16),
        scratch_shapes=[pltpu.VMEM((ATT_PAIRS, seq, 2 * LANES), BF16),
                        pltpu.VMEM((ATT_PAIRS, LANES + ONES_ROWS, seq), BF16)],
        compiler_params=_compiler_params(2),
        name="fox_attn_rowmax" if use_row_max else "fox_attn",
    )(q, qa, k, ka, v, gate)


def kernel(x, gmlp_w_in, gmlp_ln_g, gmlp_ln_b, gmlp_w_s, gmlp_b_s, gmlp_w_out, fox_w_in, fox_b_f, fox_q_g, fox_k_g, fox_w_out, mix_norm_g, mlp_norm_g, mlp_w1, mlp_w2):
    batch, seq, d = x.shape
    depth = mix_norm_g.shape[0]
    wdt = fox_w_out.shape[1]
    n_heads = wdt // HEAD_DIM
    xs = x.reshape(batch * seq, d)

    gmlp_w_in, gmlp_w_out, fox_w_in_bf, fox_w_out = (
        w.astype(BF16) for w in (gmlp_w_in, gmlp_w_out, fox_w_in, fox_w_out))
    fox_layers = list(range(1, depth, 2))
    mlp_w1_fox = _to_bf16_layers(mlp_w1, fox_layers, CONVERT_COLS)
    mlp_w2_fox = _to_bf16_layers(mlp_w2, fox_layers, CONVERT_COLS)

    for layer in range(depth):
        j = layer // 2
        if layer % 2 == 0:
            xs = _gmlp_layer(xs, mix_norm_g[layer], gmlp_w_in, gmlp_ln_g[j], gmlp_ln_b[j],
                             gmlp_w_s, gmlp_b_s[j], gmlp_w_out, j)
            xs = _mlp_layer_f32_weights(xs, mlp_norm_g[layer], mlp_w1, mlp_w2, layer)
        else:
            w_f = jnp.pad(fox_w_in[j, :, 4 * wdt:], ((0, 0), (0, LANES - n_heads))).astype(BF16)
            b_f = jnp.pad(fox_b_f[j], (0, LANES - n_heads)).reshape(1, LANES)
            q_gain = (jnp.tile(fox_q_g[j], n_heads) * (HEAD_DIM ** -0.5 * LOG2E)).reshape(1, wdt)
            k_gain = jnp.tile(fox_k_g[j], n_heads).reshape(1, wdt)
            bound = (jnp.max(jnp.abs(fox_q_g[j])) * jnp.max(jnp.abs(fox_k_g[j]))) * (HEAD_DIM * HEAD_DIM ** -0.5 * LOG2E)
            fixed_shift_ok = bound <= ATT_MAX_FIXED_SHIFT
            q, k, v, gate, qa, ka = _fox_proj(xs, mix_norm_g[layer], fox_w_in_bf, j, w_f, b_f,
                                              q_gain, k_gain, seq, jnp.where(fixed_shift_ok, bound, 0.0))
            og = lax.cond(fixed_shift_ok,
                          functools.partial(_fox_attn, batch=batch, seq=seq, use_row_max=False),
                          functools.partial(_fox_attn, batch=batch, seq=seq, use_row_max=True),
                          q, k, v, gate, qa, ka)
            xs = _mlp_layer(xs, mlp_norm_g[layer], mlp_w1_fox, mlp_w2_fox, j,
                            og=og, wo=fox_w_out, wo_layer=j)
    return xs.reshape(batch, seq, d)
```

```python
import functools
import math

import numpy as np
import jax
import jax.numpy as jnp
from jax import lax
from jax.experimental import pallas as pl
from jax.experimental.pallas import tpu as pltpu

F32 = jnp.float32
BF16 = jnp.bfloat16

CHUNK = 128
GMLP_GROUPS = 8
HEAD_DIM = 64
RMS_EPS = 1e-6
LN_EPS = 1e-5
LOG2E = math.log2(math.e)

LANES = 128
MXU_DIM = 256
VMEM_LIMIT_BYTES = 56 * 1024 * 1024

MLP_ROWS = 1024
MLP_FF_TILE = 512
CONVERT_COLS = 512
GMLP_ROWS = 1024
PROJ_ROWS = 1024
CUMSUM_ROWS = 512
PROJ_SLAB = 512
ATT_Q = 256
ATT_PAIRS = 2
ATT_AHEAD = 3
ONES_ROWS = 16
AUG_PER_HEAD = 6
SHIFT_LANES = 3
ATT_MAX_FIXED_SHIFT = 48.0
NEG_BIG = -1e30


def _compiler_params(n_axes):
    return pltpu.CompilerParams(
        dimension_semantics=("arbitrary",) * n_axes,
        vmem_limit_bytes=VMEM_LIMIT_BYTES,
    )


def _resident(shape):
    nd = len(shape)
    return pl.BlockSpec(shape, lambda *_: (0,) * nd, pipeline_mode=pl.Buffered(1))


def _layer_resident(stacked_shape, layer, width=None):
    tail = tuple(stacked_shape[1:-1]) + (width or stacked_shape[-1],)
    return pl.BlockSpec((None,) + tail, lambda *_: (layer,) + (0,) * len(tail),
                        pipeline_mode=pl.Buffered(1))


def _rms_norm(x, g):
    ms = jnp.mean(x * x, axis=-1, keepdims=True)
    return x * lax.rsqrt(ms + RMS_EPS) * g


def _gelu_tanh(x):
    c = math.sqrt(2.0 / math.pi)
    return x * (0.5 * (1.0 + jnp.tanh(c * (x + 0.044715 * (x * x * x)))))


def _dot(a, b):
    return jnp.dot(a, b, preferred_element_type=F32)


def _mlp_kernel(*refs, fused_out_proj):
    if fused_out_proj:
        x_ref, og_ref, wo_ref, g_ref, w1_ref, w2_ref, o_ref = refs
    else:
        x_ref, g_ref, w1_ref, w2_ref, o_ref = refs
    tf = MLP_FF_TILE
    x = x_ref[...]
    if fused_out_proj:
        og = jnp.concatenate([og_ref[p] for p in range(og_ref.shape[0])], axis=1)
        x = x + _dot(og, wo_ref[...])
    h = _rms_norm(x, g_ref[...]).astype(BF16)
    acc = x
    for f0 in range(0, w1_ref.shape[1], tf):
        a = jnp.maximum(_dot(h, w1_ref[:, f0:f0 + tf]), 0.0)
        acc = acc + _dot((a * a).astype(BF16), w2_ref[f0:f0 + tf, :])
    o_ref[...] = acc


def _mlp_layer(x, g, w1, w2, layer, og=None, wo=None, wo_layer=None):
    t, d = x.shape
    tm = MLP_ROWS
    fused = og is not None
    row_spec = lambda width: pl.BlockSpec((tm, width), lambda i: (i, 0))
    in_specs = [row_spec(d)]
    args = [x]
    if fused:
        in_specs += [pl.BlockSpec((og.shape[0], tm, LANES), lambda i: (0, i, 0)),
                     _layer_resident(wo.shape, wo_layer)]
        args += [og, wo]
    in_specs += [_resident((1, d)), _layer_resident(w1.shape, layer), _layer_resident(w2.shape, layer)]
    args += [g.reshape(1, d), w1, w2]
    return pl.pallas_call(
        functools.partial(_mlp_kernel, fused_out_proj=fused),
        grid=(t // tm,),
        in_specs=in_specs,
        out_specs=row_spec(d),
        out_shape=jax.ShapeDtypeStruct((t, d), F32),
        compiler_params=_compiler_params(1),
        name="mlp_fused" if fused else "mlp",
    )(*args)


def _mlp_stream_kernel(x_ref, g_ref, w1f_ref, w2f_ref, o_ref, w1_ref, w2_ref, h_ref, *, n_chunks):
    s = pl.program_id(0)

    @pl.when(s < n_chunks)
    def _():
        w1c = w1f_ref[...].astype(BF16)
        w2c = w2f_ref[...].astype(BF16)
        w1_ref[s] = w1c
        w2_ref[s] = w2c

        @pl.when(s == 0)
        def _():
            x = x_ref[...]
            h_ref[...] = _rms_norm(x, g_ref[...]).astype(BF16)
            o_ref[...] = x

        a = jnp.maximum(_dot(h_ref[...], w1c), 0.0)
        o_ref[...] += _dot((a * a).astype(BF16), w2c)

    @pl.when(s >= n_chunks)
    def _():
        x = x_ref[...]
        h = _rms_norm(x, g_ref[...]).astype(BF16)
        acc = x
        for j in range(n_chunks):
            a = jnp.maximum(_dot(h, w1_ref[j]), 0.0)
            acc = acc + _dot((a * a).astype(BF16), w2_ref[j])
        o_ref[...] = acc


def _convert_kernel(src_ref, dst_ref):
    dst_ref[...] = src_ref[...].astype(dst_ref.dtype)


def _to_bf16_layers(w, layers, col_block):
    _, rows, cols = w.shape
    first, step = layers[0], layers[1] - layers[0]
    assert list(layers) == [first + step * n for n in range(len(layers))]
    return pl.pallas_call(
        _convert_kernel,
        grid=(len(layers), cols // col_block),
        in_specs=[pl.BlockSpec((None, rows, col_block), lambda n, j: (first + step * n, 0, j))],
        out_specs=pl.BlockSpec((None, rows, col_block), lambda n, j: (n, 0, j)),
        out_shape=jax.ShapeDtypeStruct((len(layers), rows, cols), BF16),
        compiler_params=_compiler_params(2),
        name="to_bf16",
    )(w)


def _mlp_layer_f32_weights(x, g, w1, w2, layer):
    t, d = x.shape
    ff = w1.shape[2]
    tm, tf = MLP_ROWS, MLP_FF_TILE
    n_chunks = ff // tf
    last = n_chunks - 1
    row_spec = pl.BlockSpec((tm, d), lambda s: (jnp.maximum(s - last, 0), 0))
    return pl.pallas_call(
        functools.partial(_mlp_stream_kernel, n_chunks=n_chunks),
        grid=(n_chunks + t // tm - 1,),
        in_specs=[
            row_spec,
            _resident((1, d)),
            pl.BlockSpec((None, d, tf), lambda s: (layer, 0, jnp.minimum(s, last))),
            pl.BlockSpec((None, tf, d), lambda s: (layer, jnp.minimum(s, last), 0)),
        ],
        out_specs=row_spec,
        out_shape=jax.ShapeDtypeStruct((t, d), F32),
        scratch_shapes=[pltpu.VMEM((n_chunks, d, tf), BF16), pltpu.VMEM((n_chunks, tf, d), BF16),
                        pltpu.VMEM((tm, d), BF16)],
        compiler_params=_compiler_params(1),
        name="mlp_f32w",
    )(x, g.reshape(1, d), w1, w2)


def _gmlp_kernel(x_ref, g_ref, win_ref, lng_ref, lnb_ref, ws_ref, bst_ref, wout_ref,
                 o_ref, vn_ref, gated_ref):
    tm = x_ref.shape[0]
    e = lng_ref.shape[1]
    gd = e // GMLP_GROUPS
    x = x_ref[...]
    h = _rms_norm(x, g_ref[...]).astype(BF16)

    v = _gelu_tanh(_dot(h, win_ref[:, e:]))
    u_all = _gelu_tanh(_dot(h, win_ref[:, :e]))
    mu = jnp.mean(v, axis=-1, keepdims=True)
    var = jnp.mean(v * v, axis=-1, keepdims=True) - mu * mu
    vn_ref[...] = ((v - mu) * lax.rsqrt(var + LN_EPS) * lng_ref[...] + lnb_ref[...]).astype(BF16)

    row = lax.broadcasted_iota(jnp.int32, (CHUNK, CHUNK), 0)
    col = lax.broadcasted_iota(jnp.int32, (CHUNK, CHUNK), 1)
    causal = row >= col
    for grp in range(GMLP_GROUPS):
        cols = slice(grp * gd, (grp + 1) * gd)
        w = jnp.where(causal, ws_ref[grp], 0.0).astype(BF16)
        bias = bst_ref[:, grp:grp + 1]
        for c in range(tm // CHUNK):
            rows = slice(c * CHUNK, (c + 1) * CHUNK)
            s = _dot(w, vn_ref[rows, cols]) + bias
            gated_ref[rows, cols] = (u_all[rows, cols] * s).astype(BF16)

    o_ref[...] = x + _dot(gated_ref[...], wout_ref[...])


def _gmlp_layer(x, g, w_in, ln_g, ln_b, w_s, b_s, w_out, layer):
    t, d = x.shape
    e = ln_g.shape[0]
    tm = GMLP_ROWS
    return pl.pallas_call(
        _gmlp_kernel,
        grid=(t // tm,),
        in_specs=[
            pl.BlockSpec((tm, d), lambda i: (i, 0)),
            _resident((1, d)),
            _layer_resident(w_in.shape, layer),
            _resident((1, e)),
            _resident((1, e)),
            _layer_resident(w_s.shape, layer),
            _resident((CHUNK, GMLP_GROUPS)),
            _layer_resident(w_out.shape, layer),
        ],
        out_specs=pl.BlockSpec((tm, d), lambda i: (i, 0)),
        out_shape=jax.ShapeDtypeStruct((t, d), F32),
        scratch_shapes=[pltpu.VMEM((tm, e), BF16), pltpu.VMEM((tm, e), BF16)],
        compiler_params=_compiler_params(1),
        name="gmlp",
    )(x, g.reshape(1, d), w_in, ln_g.reshape(1, e), ln_b.reshape(1, e), w_s, b_s.T, w_out)


def _split3(x):
    hi = x.astype(BF16)
    r = x - hi.astype(F32)
    mid = r.astype(BF16)
    lo = (r - mid.astype(F32)).astype(BF16)
    return hi, mid, lo


def _store_pairs(dst_ref, col0, val):
    for c in range(0, val.shape[1], LANES):
        dst_ref[(col0 + c) // LANES] = val[:, c:c + LANES]


def _fox_proj_kernel(x_ref, g_ref, w_ref, wf_ref, bf_ref, qg_ref, kg_ref, gmat_ref,
                     eqk_ref, oneq_ref, onek_ref,
                     q_ref, k_ref, v_ref, gate_ref, qa_ref, ka_ref, carry_ref,
                     *, blocks_per_seq):
    tm = x_ref.shape[0]
    wdt = qg_ref.shape[1]
    i = pl.program_id(0)
    h = _rms_norm(x_ref[...], g_ref[...]).astype(BF16)

    gmat = gmat_ref[...]

    def normed_heads(base, gain_ref, dst):
        for s0 in range(0, wdt, PROJ_SLAB):
            y = _dot(h, w_ref[:, base + s0: base + s0 + PROJ_SLAB])
            for c0 in range(0, PROJ_SLAB, MXU_DIM):
                cols = slice(s0 + c0, s0 + c0 + MXU_DIM)
                yy = y[:, c0:c0 + MXU_DIM]
                ms = _dot((yy * yy).astype(BF16), gmat)
                _store_pairs(dst, s0 + c0, (yy * lax.rsqrt(ms + RMS_EPS) * gain_ref[:, cols]).astype(BF16))

    f = _dot(h, wf_ref[...]) + bf_ref[...]
    logf = jnp.minimum(f, 0.0) - jnp.log(1.0 + jnp.exp(-jnp.abs(f)))
    normed_heads(0, qg_ref, q_ref)

    cr = min(tm, CUMSUM_ROWS)
    row = lax.broadcasted_iota(jnp.int32, (cr, cr), 0)
    col = lax.broadcasted_iota(jnp.int32, (cr, cr), 1)
    tril = jnp.where(row >= col, 1.0, 0.0).astype(BF16)
    first = (i % blocks_per_seq) == 0
    carry = jnp.where(first, 0.0, carry_ref[0:1, :])
    c_parts = []
    for r0 in range(0, tm, cr):
        csum = _dot(tril, jnp.concatenate(_split3(logf[r0:r0 + cr]), axis=1))
        part = (csum[:, :LANES] + csum[:, LANES:2 * LANES]) + csum[:, 2 * LANES:] + carry
        carry = part[cr - 1:cr, :]
        c_parts.append(part)
    c = jnp.concatenate(c_parts, axis=0)
    carry_ref[0:1, :] = carry
    normed_heads(wdt, kg_ref, k_ref)

    pieces = jnp.concatenate(_split3(c * LOG2E), axis=1)
    slabs = _dot(pieces, eqk_ref[...])
    qa_ref[...] = (slabs[:, :LANES] + oneq_ref[...]).astype(BF16)
    ka_ref[...] = (onek_ref[...] - slabs[:, LANES:]).astype(BF16)

    for s0 in range(0, wdt, PROJ_SLAB):
        _store_pairs(v_ref, s0, _dot(h, w_ref[:, 2 * wdt + s0: 2 * wdt + s0 + PROJ_SLAB]).astype(BF16))
        gt = _dot(h, w_ref[:, 3 * wdt + s0: 3 * wdt + s0 + PROJ_SLAB])
        _store_pairs(gate_ref, s0, (1.0 / (1.0 + jnp.exp(-gt))).astype(BF16))


def _bias_slab_constants(n_heads):
    eqk = np.zeros((3 * LANES, 2 * LANES), np.float32)
    oneq = np.zeros((1, LANES), np.float32)
    onek = np.zeros((1, LANES), np.float32)
    for h in range(n_heads):
        for p in range(3):
            eqk[p * LANES + h, AUG_PER_HEAD * h + p] = 1.0
            eqk[p * LANES + h, LANES + AUG_PER_HEAD * h + 3 + p] = 1.0
            oneq[0, AUG_PER_HEAD * h + 3 + p] = 1.0
            onek[0, AUG_PER_HEAD * h + p] = 1.0
    onek[0, AUG_PER_HEAD * n_heads: AUG_PER_HEAD * n_heads + SHIFT_LANES] = 1.0
    return jnp.asarray(eqk, BF16), jnp.asarray(oneq), jnp.asarray(onek)


def _head_mean_matrix():
    idx = np.arange(MXU_DIM) // HEAD_DIM
    return jnp.asarray((idx[:, None] == idx[None, :]).astype(np.float32) / HEAD_DIM, BF16)


def _fox_proj(x, g, w_in, layer, w_f, b_f, q_gain, k_gain, seq, logit_shift):
    t, d = x.shape
    wdt = q_gain.shape[1]
    n_heads = wdt // HEAD_DIM
    tm = PROJ_ROWS
    eqk, oneq, onek = _bias_slab_constants(n_heads)
    shift_pieces = jnp.stack([p.astype(F32) for p in _split3(-logit_shift)])
    oneq = lax.dynamic_update_slice(oneq, shift_pieces.reshape(1, SHIFT_LANES), (0, AUG_PER_HEAD * n_heads))
    gmat = _head_mean_matrix()
    row_spec = lambda width: pl.BlockSpec((tm, width), lambda i: (i, 0))
    bf = lambda width: jax.ShapeDtypeStruct((t, width), BF16)
    pairs = wdt // LANES
    pair_spec = pl.BlockSpec((pairs, tm, LANES), lambda i: (0, i, 0))
    pair_shape = jax.ShapeDtypeStruct((pairs, t, LANES), BF16)
    return pl.pallas_call(
        functools.partial(_fox_proj_kernel, blocks_per_seq=seq // tm),
        grid=(t // tm,),
        in_specs=[
            row_spec(d), _resident((1, d)), _layer_resident(w_in.shape, layer, width=4 * wdt),
            _resident(w_f.shape),
            _resident((1, LANES)), _resident((1, wdt)), _resident((1, wdt)),
            _resident(gmat.shape), _resident(eqk.shape),
            _resident((1, LANES)), _resident((1, LANES)),
        ],
        out_specs=[pair_spec] * 4 + [row_spec(LANES), row_spec(LANES)],
        out_shape=[pair_shape] * 4 + [bf(LANES), bf(LANES)],
        scratch_shapes=[pltpu.VMEM((8, LANES), F32)],
        compiler_params=_compiler_params(1),
        name="fox_proj",
    )(x, g.reshape(1, d), w_in, w_f, b_f, q_gain, k_gain, gmat, eqk, oneq, onek)


def _nt_dot(a, b):
    return lax.dot_general(a, b, (((1,), (1,)), ((), ())), preferred_element_type=F32)


def _fox_attn_kernel(q_ref, qa_ref, k_ref, ka_ref, v_ref, gate_ref, o_ref, kcat_ref, vt_ref,
                     *, use_row_max, shift_lo):
    seq = k_ref.shape[1]
    tq = ATT_Q
    group = pl.program_id(1)
    lane = lax.broadcasted_iota(jnp.int32, (1, LANES), 1)

    key_idx = lax.broadcasted_iota(jnp.int32, (tq, 2 * tq), 0)
    qry_idx = lax.broadcasted_iota(jnp.int32, (tq, 2 * tq), 1)
    visible = key_idx <= qry_idx - jnp.where(qry_idx >= tq, tq, 0)

    head_masks = []
    for pp in range(ATT_PAIRS):
        kcat_ref[pp, :, :LANES] = k_ref[pp]
        kcat_ref[pp, :, LANES:] = ka_ref[...]
        vt_ref[pp, :LANES, :] = v_ref[pp].astype(F32).T.astype(BF16)
        vt_ref[pp, LANES:, :] = jnp.ones((ONES_ROWS, seq), BF16)
        masks = []
        for e in range(2):
            in_half = (lane >= HEAD_DIM * e) & (lane < HEAD_DIM * (e + 1))
            aug_lo = AUG_PER_HEAD * (2 * (ATT_PAIRS * group + pp) + e)
            in_aug = ((lane >= aug_lo) & (lane < aug_lo + AUG_PER_HEAD)) | (
                (lane >= shift_lo) & (lane < shift_lo + SHIFT_LANES))
            masks.append((jnp.where(in_half, 1.0, 0.0).astype(BF16),
                          jnp.where(in_aug, 1.0, 0.0).astype(BF16)))
        head_masks.append(masks)

    def scores(pp, i):
        r0, r1 = i * tq, (i + 1) * tq
        q = q_ref[pp, r0:r1, :]
        qa = qa_ref[r0:r1, :]
        qboth = jnp.concatenate(
            [jnp.concatenate([q * half, qa * aug], axis=1) for half, aug in head_masks[pp]], axis=0)
        s_diag = jnp.where(visible, _nt_dot(kcat_ref[pp, r0:r1, :], qboth), NEG_BIG)
        s_past = _nt_dot(kcat_ref[pp, :r0, :], qboth) if i > 0 else None
        if use_row_max:
            m = jnp.max(s_diag, axis=0, keepdims=True)
            if i > 0:
                m = jnp.maximum(m, jnp.max(s_past, axis=0, keepdims=True))
                s_past = s_past - m
            s_diag = s_diag - m
        p_past = jnp.exp2(s_past).astype(BF16) if i > 0 else None
        return p_past, jnp.exp2(s_diag).astype(BF16)

    def finish(pp, i, p_past, p_diag):
        r0, r1 = i * tq, (i + 1) * tq
        acc = _dot(vt_ref[pp, :, r0:r1], p_diag)
        if i > 0:
            acc = acc + _dot(vt_ref[pp, :, :r0], p_past)
        inv = 1.0 / acc[LANES:LANES + 1, :]
        o = jnp.concatenate([acc[:HEAD_DIM, :tq] * inv[:, :tq],
                             acc[HEAD_DIM:LANES, tq:] * inv[:, tq:]], axis=0).T
        o_ref[pp, r0:r1, :] = (o * gate_ref[pp, r0:r1, :].astype(F32)).astype(BF16)

    order = [(pp, i) for pp in range(ATT_PAIRS) for i in range(seq // tq - 1, -1, -1)]
    pending = [scores(*blk) for blk in order[:ATT_AHEAD]]
    for n, blk in enumerate(order):
        if n + ATT_AHEAD < len(order):
            pending.append(scores(*order[n + ATT_AHEAD]))
        finish(*blk, *pending.pop(0))


def _fox_attn(q, k, v, gate, qa, ka, batch, seq, use_row_max):
    pairs, t, _ = q.shape
    pair_spec = pl.BlockSpec((ATT_PAIRS, seq, LANES), lambda b, p: (p, b, 0))
    aug_spec = pl.BlockSpec((seq, LANES), lambda b, p: (b, 0))
    return pl.pallas_call(
        functools.partial(_fox_attn_kernel, use_row_max=use_row_max, shift_lo=2 * pairs * AUG_PER_HEAD),
        grid=(batch, pairs // ATT_PAIRS),
        in_specs=[pair_spec, aug_spec, pair_spec, aug_spec, pair_spec, pair_spec],
        out_specs=pair_spec,
        out_shape=jax.ShapeDtypeStruct((pairs, t, LANES), BF16),
        scratch_shapes=[pltpu.VMEM((ATT_PAIRS, seq, 2 * LANES), BF16),
                        pltpu.VMEM((ATT_PAIRS, LANES + ONES_ROWS, seq), BF16)],
        compiler_params=_compiler_params(2),
        name="fox_attn_rowmax" if use_row_max else "fox_attn",
    )(q, qa, k, ka, v, gate)


def kernel(x, gmlp_w_in, gmlp_ln_g, gmlp_ln_b, gmlp_w_s, gmlp_b_s, gmlp_w_out, fox_w_in, fox_b_f, fox_q_g, fox_k_g, fox_w_out, mix_norm_g, mlp_norm_g, mlp_w1, mlp_w2):
    batch, seq, d = x.shape
    depth = mix_norm_g.shape[0]
    wdt = fox_w_out.shape[1]
    n_heads = wdt // HEAD_DIM
    xs = x.reshape(batch * seq, d)

    gmlp_w_in, gmlp_w_out, fox_w_in_bf, fox_w_out = (
        w.astype(BF16) for w in (gmlp_w_in, gmlp_w_out, fox_w_in, fox_w_out))
    fox_layers = list(range(1, depth, 2))
    mlp_w1_fox = _to_bf16_layers(mlp_w1, fox_layers, CONVERT_COLS)
    mlp_w2_fox = _to_bf16_layers(mlp_w2, fox_layers, CONVERT_COLS)

    for layer in range(depth):
        j = layer // 2
        if layer % 2 == 0:
            xs = _gmlp_layer(xs, mix_norm_g[layer], gmlp_w_in, gmlp_ln_g[j], gmlp_ln_b[j],
                             gmlp_w_s, gmlp_b_s[j], gmlp_w_out, j)
            xs = _mlp_layer_f32_weights(xs, mlp_norm_g[layer], mlp_w1, mlp_w2, layer)
        else:
            w_f = jnp.pad(fox_w_in[j, :, 4 * wdt:], ((0, 0), (0, LANES - n_heads))).astype(BF16)
            b_f = jnp.pad(fox_b_f[j], (0, LANES - n_heads)).reshape(1, LANES)
            q_gain = (jnp.tile(fox_q_g[j], n_heads) * (HEAD_DIM ** -0.5 * LOG2E)).reshape(1, wdt)
            k_gain = jnp.tile(fox_k_g[j], n_heads).reshape(1, wdt)
            bound = (jnp.max(jnp.abs(fox_q_g[j])) * jnp.max(jnp.abs(fox_k_g[j]))) * (HEAD_DIM * HEAD_DIM ** -0.5 * LOG2E)
            fixed_shift_ok = bound <= ATT_MAX_FIXED_SHIFT
            q, k, v, gate, qa, ka = _fox_proj(xs, mix_norm_g[layer], fox_w_in_bf, j, w_f, b_f,
                                              q_gain, k_gain, seq, jnp.where(fixed_shift_ok, bound, 0.0))
            og = lax.cond(fixed_shift_ok,
                          functools.partial(_fox_attn, batch=batch, seq=seq, use_row_max=False),
                          functools.partial(_fox_attn, batch=batch, seq=seq, use_row_max=True),
                          q, k, v, gate, qa, ka)
            xs = _mlp_layer(xs, mlp_norm_g[layer], mlp_w1_fox, mlp_w2_fox, j,
                            og=og, wo=fox_w_out, wo_layer=j)
    return xs.reshape(batch, seq, d)
```

```python
import functools
import math

import numpy as np
import jax
import jax.numpy as jnp
from jax import lax
from jax.experimental import pallas as pl
from jax.experimental.pallas import tpu as pltpu

F32 = jnp.float32
BF16 = jnp.bfloat16

CHUNK = 128
GMLP_GROUPS = 8
HEAD_DIM = 64
RMS_EPS = 1e-6
LN_EPS = 1e-5
LOG2E = math.log2(math.e)

LANES = 128
MXU_DIM = 256
VMEM_LIMIT_BYTES = 56 * 1024 * 1024

MLP_ROWS = 1024
MLP_FF_TILE = 512
CONVERT_BLOCK_BYTES = 8 * 1024 * 1024
GMLP_ROWS = 1024
PROJ_ROWS = 1024
CUMSUM_ROWS = 512
PROJ_SLAB = 512
ATT_Q = 256
ATT_PAIRS = 2
ATT_AHEAD = 3
ONES_ROWS = 16
AUG_PER_HEAD = 6
SHIFT_LANES = 3
ATT_MAX_FIXED_SHIFT = 48.0
NEG_BIG = -1e30


def _compiler_params(n_axes):
    return pltpu.CompilerParams(
        dimension_semantics=("arbitrary",) * n_axes,
        vmem_limit_bytes=VMEM_LIMIT_BYTES,
    )


def _resident(shape):
    nd = len(shape)
    return pl.BlockSpec(shape, lambda *_: (0,) * nd, pipeline_mode=pl.Buffered(1))


def _layer_resident(stacked_shape, layer, width=None):
    tail = tuple(stacked_shape[1:-1]) + (width or stacked_shape[-1],)
    return pl.BlockSpec((None,) + tail, lambda *_: (layer,) + (0,) * len(tail),
                        pipeline_mode=pl.Buffered(1))


def _rms_norm(x, g):
    ms = jnp.mean(x * x, axis=-1, keepdims=True)
    return x * lax.rsqrt(ms + RMS_EPS) * g


def _gelu_tanh(x):
    c = math.sqrt(2.0 / math.pi)
    inner = x * (c + (c * 0.044715) * (x * x))
    half_x = 0.5 * x
    return half_x + half_x * jnp.tanh(inner)


def _dot(a, b):
    return jnp.dot(a, b, preferred_element_type=F32)


def _mlp_fused_kernel(x_ref, og_ref, wo_ref, g_ref, w1_ref, w2_ref, o_ref):
    tf = MLP_FF_TILE
    og = jnp.concatenate([og_ref[p] for p in range(og_ref.shape[0])], axis=1)
    x = x_ref[...] + _dot(og, wo_ref[...])
    h = _rms_norm(x, g_ref[...]).astype(BF16)
    acc = x
    for f0 in range(0, w1_ref.shape[1], tf):
        a = jnp.maximum(_dot(h, w1_ref[:, f0:f0 + tf]), 0.0)
        acc = acc + _dot((a * a).astype(BF16), w2_ref[f0:f0 + tf, :])
    o_ref[...] = acc


def _mlp_layer_fused(x, g, w1, w2, og, wo, layer):
    t, d = x.shape
    tm = MLP_ROWS
    row_spec = pl.BlockSpec((tm, d), lambda i: (i, 0))
    return pl.pallas_call(
        _mlp_fused_kernel,
        grid=(t // tm,),
        in_specs=[row_spec,
                  pl.BlockSpec((og.shape[0], tm, LANES), lambda i: (0, i, 0)),
                  _layer_resident(wo.shape, layer),
                  _resident((1, d)),
                  _layer_resident(w1.shape, layer),
                  _layer_resident(w2.shape, layer)],
        out_specs=row_spec,
        out_shape=jax.ShapeDtypeStruct((t, d), F32),
        compiler_params=_compiler_params(1),
        name="mlp_fused",
    )(x, og, wo, g.reshape(1, d), w1, w2)


def _mlp_stream_kernel(x_ref, g_ref, w1f_ref, w2f_ref, o_ref, w1_ref, w2_ref, h_ref, *, n_chunks):
    s = pl.program_id(0)

    @pl.when(s < n_chunks)
    def _():
        w1c = w1f_ref[...].astype(BF16)
        w2c = w2f_ref[...].astype(BF16)
        w1_ref[s] = w1c
        w2_ref[s] = w2c

        @pl.when(s == 0)
        def _():
            x = x_ref[...]
            h_ref[...] = _rms_norm(x, g_ref[...]).astype(BF16)
            o_ref[...] = x

        a = jnp.maximum(_dot(h_ref[...], w1c), 0.0)
        o_ref[...] += _dot((a * a).astype(BF16), w2c)

    @pl.when(s >= n_chunks)
    def _():
        x = x_ref[...]
        h = _rms_norm(x, g_ref[...]).astype(BF16)
        acc = x
        for j in range(n_chunks):
            a = jnp.maximum(_dot(h, w1_ref[j]), 0.0)
            acc = acc + _dot((a * a).astype(BF16), w2_ref[j])
        o_ref[...] = acc


def _convert_kernel(src_ref, dst_ref):
    dst_ref[...] = src_ref[...].astype(dst_ref.dtype)


def _to_bf16_layers(w, layers):
    _, rows, cols = w.shape
    col_block = min(cols, CONVERT_BLOCK_BYTES // (rows * w.dtype.itemsize))
    first, step = layers[0], layers[1] - layers[0]
    assert list(layers) == [first + step * n for n in range(len(layers))]
    return pl.pallas_call(
        _convert_kernel,
        grid=(len(layers), cols // col_block),
        in_specs=[pl.BlockSpec((None, rows, col_block), lambda n, j: (first + step * n, 0, j))],
        out_specs=pl.BlockSpec((None, rows, col_block), lambda n, j: (n, 0, j)),
        out_shape=jax.ShapeDtypeStruct((len(layers), rows, cols), BF16),
        compiler_params=_compiler_params(2),
        name="to_bf16",
    )(w)


def _mlp_layer_f32_weights(x, g, w1, w2, layer):
    t, d = x.shape
    ff = w1.shape[2]
    tm, tf = MLP_ROWS, MLP_FF_TILE
    n_chunks = ff // tf
    last = n_chunks - 1
    row_spec = pl.BlockSpec((tm, d), lambda s: (jnp.maximum(s - last, 0), 0))
    return pl.pallas_call(
        functools.partial(_mlp_stream_kernel, n_chunks=n_chunks),
        grid=(n_chunks + t // tm - 1,),
        in_specs=[
            row_spec,
            _resident((1, d)),
            pl.BlockSpec((None, d, tf), lambda s: (layer, 0, jnp.minimum(s, last))),
            pl.BlockSpec((None, tf, d), lambda s: (layer, jnp.minimum(s, last), 0)),
        ],
        out_specs=row_spec,
        out_shape=jax.ShapeDtypeStruct((t, d), F32),
        scratch_shapes=[pltpu.VMEM((n_chunks, d, tf), BF16), pltpu.VMEM((n_chunks, tf, d), BF16),
                        pltpu.VMEM((tm, d), BF16)],
        compiler_params=_compiler_params(1),
        name="mlp_f32w",
    )(x, g.reshape(1, d), w1, w2)


def _gmlp_kernel(x_ref, g_ref, win_ref, lng_ref, lnb_ref, ws_ref, bst_ref, wout_ref,
                 o_ref, vn_ref, gated_ref):
    tm = x_ref.shape[0]
    e = lng_ref.shape[1]
    gd = e // GMLP_GROUPS
    x = x_ref[...]
    h = _rms_norm(x, g_ref[...]).astype(BF16)

    v = _gelu_tanh(_dot(h, win_ref[:, e:]))
    u_all = _gelu_tanh(_dot(h, win_ref[:, :e]))
    mu = jnp.mean(v, axis=-1, keepdims=True)
    var = jnp.mean(v * v, axis=-1, keepdims=True) - mu * mu
    vn_ref[...] = ((v - mu) * lax.rsqrt(var + LN_EPS) * lng_ref[...] + lnb_ref[...]).astype(BF16)

    row = lax.broadcasted_iota(jnp.int32, (CHUNK, CHUNK), 0)
    col = lax.broadcasted_iota(jnp.int32, (CHUNK, CHUNK), 1)
    causal = row >= col
    for grp in range(GMLP_GROUPS):
        cols = slice(grp * gd, (grp + 1) * gd)
        w = jnp.where(causal, ws_ref[grp], 0.0).astype(BF16)
        bias = bst_ref[:, grp:grp + 1]
        for c in range(tm // CHUNK):
            rows = slice(c * CHUNK, (c + 1) * CHUNK)
            s = _dot(w, vn_ref[rows, cols]) + bias
            gated_ref[rows, cols] = (u_all[rows, cols] * s).astype(BF16)

    o_ref[...] = x + _dot(gated_ref[...], wout_ref[...])


def _gmlp_layer(x, g, w_in, ln_g, ln_b, w_s, b_s, w_out, layer):
    t, d = x.shape
    e = ln_g.shape[0]
    tm = GMLP_ROWS
    return pl.pallas_call(
        _gmlp_kernel,
        grid=(t // tm,),
        in_specs=[
            pl.BlockSpec((tm, d), lambda i: (i, 0)),
            _resident((1, d)),
            _layer_resident(w_in.shape, layer),
            _resident((1, e)),
            _resident((1, e)),
            _layer_resident(w_s.shape, layer),
            _resident((CHUNK, GMLP_GROUPS)),
            _layer_resident(w_out.shape, layer),
        ],
        out_specs=pl.BlockSpec((tm, d), lambda i: (i, 0)),
        out_shape=jax.ShapeDtypeStruct((t, d), F32),
        scratch_shapes=[pltpu.VMEM((tm, e), BF16), pltpu.VMEM((tm, e), BF16)],
        compiler_params=_compiler_params(1),
        name="gmlp",
    )(x, g.reshape(1, d), w_in, ln_g.reshape(1, e), ln_b.reshape(1, e), w_s, b_s.T, w_out)


def _split3(x):
    hi = x.astype(BF16)
    r = x - hi.astype(F32)
    mid = r.astype(BF16)
    lo = (r - mid.astype(F32)).astype(BF16)
    return hi, mid, lo


def _store_pairs(dst_ref, col0, val):
    for c in range(0, val.shape[1], LANES):
        dst_ref[(col0 + c) // LANES] = val[:, c:c + LANES]


def _fox_proj_kernel(x_ref, g_ref, w_ref, wf_ref, bf_ref, qg_ref, kg_ref, gmat_ref,
                     eqk_ref, oneq_ref, onek_ref,
                     q_ref, k_ref, v_ref, gate_ref, qa_ref, ka_ref, carry_ref,
                     *, blocks_per_seq):
    tm = x_ref.shape[0]
    wdt = qg_ref.shape[1]
    i = pl.program_id(0)
    h = _rms_norm(x_ref[...], g_ref[...]).astype(BF16)

    gmat = gmat_ref[...]

    def normed_heads(base, gain_ref, dst):
        for s0 in range(0, wdt, PROJ_SLAB):
            y = _dot(h, w_ref[:, base + s0: base + s0 + PROJ_SLAB])
            for c0 in range(0, PROJ_SLAB, MXU_DIM):
                cols = slice(s0 + c0, s0 + c0 + MXU_DIM)
                yy = y[:, c0:c0 + MXU_DIM]
                ms = _dot((yy * yy).astype(BF16), gmat)
                _store_pairs(dst, s0 + c0, (yy * lax.rsqrt(ms + RMS_EPS) * gain_ref[:, cols]).astype(BF16))

    f = _dot(h, wf_ref[...]) + bf_ref[...]
    logf = jnp.minimum(f, 0.0) - jnp.log(1.0 + jnp.exp(-jnp.abs(f)))
    normed_heads(0, qg_ref, q_ref)

    cr = min(tm, CUMSUM_ROWS)
    row = lax.broadcasted_iota(jnp.int32, (cr, cr), 0)
    col = lax.broadcasted_iota(jnp.int32, (cr, cr), 1)
    tril = jnp.where(row >= col, 1.0, 0.0).astype(BF16)
    first = (i % blocks_per_seq) == 0
    carry = jnp.where(first, 0.0, carry_ref[0:1, :])
    c_parts = []
    for r0 in range(0, tm, cr):
        csum = _dot(tril, jnp.concatenate(_split3(logf[r0:r0 + cr]), axis=1))
        part = (csum[:, :LANES] + csum[:, LANES:2 * LANES]) + csum[:, 2 * LANES:] + carry
        carry = part[cr - 1:cr, :]
        c_parts.append(part)
    c = jnp.concatenate(c_parts, axis=0)
    carry_ref[0:1, :] = carry
    normed_heads(wdt, kg_ref, k_ref)

    pieces = jnp.concatenate(_split3(c * LOG2E), axis=1)
    slabs = _dot(pieces, eqk_ref[...])
    qa_ref[...] = (slabs[:, :LANES] + oneq_ref[...]).astype(BF16)
    ka_ref[...] = (onek_ref[...] - slabs[:, LANES:]).astype(BF16)

    for s0 in range(0, wdt, PROJ_SLAB):
        _store_pairs(v_ref, s0, _dot(h, w_ref[:, 2 * wdt + s0: 2 * wdt + s0 + PROJ_SLAB]).astype(BF16))
        gt = _dot(h, w_ref[:, 3 * wdt + s0: 3 * wdt + s0 + PROJ_SLAB])
        _store_pairs(gate_ref, s0, (1.0 / (1.0 + jnp.exp(-gt))).astype(BF16))


def _bias_slab_constants(n_heads):
    eqk = np.zeros((3 * LANES, 2 * LANES), np.float32)
    oneq = np.zeros((1, LANES), np.float32)
    onek = np.zeros((1, LANES), np.float32)
    for h in range(n_heads):
        for p in range(3):
            eqk[p * LANES + h, AUG_PER_HEAD * h + p] = 1.0
            eqk[p * LANES + h, LANES + AUG_PER_HEAD * h + 3 + p] = 1.0
            oneq[0, AUG_PER_HEAD * h + 3 + p] = 1.0
            onek[0, AUG_PER_HEAD * h + p] = 1.0
    onek[0, AUG_PER_HEAD * n_heads: AUG_PER_HEAD * n_heads + SHIFT_LANES] = 1.0
    return jnp.asarray(eqk, BF16), jnp.asarray(oneq), jnp.asarray(onek)


def _head_mean_matrix():
    idx = np.arange(MXU_DIM) // HEAD_DIM
    return jnp.asarray((idx[:, None] == idx[None, :]).astype(np.float32) / HEAD_DIM, BF16)


def _fox_proj(x, g, w_in, layer, w_f, b_f, q_gain, k_gain, seq, logit_shift):
    t, d = x.shape
    wdt = q_gain.shape[1]
    n_heads = wdt // HEAD_DIM
    tm = PROJ_ROWS
    eqk, oneq, onek = _bias_slab_constants(n_heads)
    shift_pieces = jnp.stack([p.astype(F32) for p in _split3(-logit_shift)])
    oneq = lax.dynamic_update_slice(oneq, shift_pieces.reshape(1, SHIFT_LANES), (0, AUG_PER_HEAD * n_heads))
    gmat = _head_mean_matrix()
    row_spec = lambda width: pl.BlockSpec((tm, width), lambda i: (i, 0))
    bf = lambda width: jax.ShapeDtypeStruct((t, width), BF16)
    pairs = wdt // LANES
    pair_spec = pl.BlockSpec((pairs, tm, LANES), lambda i: (0, i, 0))
    pair_shape = jax.ShapeDtypeStruct((pairs, t, LANES), BF16)
    return pl.pallas_call(
        functools.partial(_fox_proj_kernel, blocks_per_seq=seq // tm),
        grid=(t // tm,),
        in_specs=[
            row_spec(d), _resident((1, d)), _layer_resident(w_in.shape, layer, width=4 * wdt),
            _resident(w_f.shape),
            _resident((1, LANES)), _resident((1, wdt)), _resident((1, wdt)),
            _resident(gmat.shape), _resident(eqk.shape),
            _resident((1, LANES)), _resident((1, LANES)),
        ],
        out_specs=[pair_spec] * 4 + [row_spec(LANES), row_spec(LANES)],
        out_shape=[pair_shape] * 4 + [bf(LANES), bf(LANES)],
        scratch_shapes=[pltpu.VMEM((8, LANES), F32)],
        compiler_params=_compiler_params(1),
        name="fox_proj",
    )(x, g.reshape(1, d), w_in, w_f, b_f, q_gain, k_gain, gmat, eqk, oneq, onek)


def _nt_dot(a, b):
    return lax.dot_general(a, b, (((1,), (1,)), ((), ())), preferred_element_type=F32)


def _fox_attn_kernel(q_ref, qa_ref, k_ref, ka_ref, v_ref, gate_ref, o_ref, kcat_ref, vt_ref,
                     *, use_row_max, shift_lo):
    seq = k_ref.shape[1]
    tq = ATT_Q
    group = pl.program_id(1)
    lane = lax.broadcasted_iota(jnp.int32, (1, LANES), 1)

    key_idx = lax.broadcasted_iota(jnp.int32, (tq, 2 * tq), 0)
    qry_idx = lax.broadcasted_iota(jnp.int32, (tq, 2 * tq), 1)
    visible = key_idx <= qry_idx - jnp.where(qry_idx >= tq, tq, 0)

    head_masks = []
    for pp in range(ATT_PAIRS):
        kcat_ref[pp, :, :LANES] = k_ref[pp]
        kcat_ref[pp, :, LANES:] = ka_ref[...]
        vt_ref[pp, :LANES, :] = v_ref[pp].astype(F32).T.astype(BF16)
        vt_ref[pp, LANES:, :] = jnp.ones((ONES_ROWS, seq), BF16)
        masks = []
        for e in range(2):
            in_half = (lane >= HEAD_DIM * e) & (lane < HEAD_DIM * (e + 1))
            aug_lo = AUG_PER_HEAD * (2 * (ATT_PAIRS * group + pp) + e)
            in_aug = ((lane >= aug_lo) & (lane < aug_lo + AUG_PER_HEAD)) | (
                (lane >= shift_lo) & (lane < shift_lo + SHIFT_LANES))
            masks.append((jnp.where(in_half, 1.0, 0.0).astype(BF16),
                          jnp.where(in_aug, 1.0, 0.0).astype(BF16)))
        head_masks.append(masks)

    def scores(pp, i):
        r0, r1 = i * tq, (i + 1) * tq
        q = q_ref[pp, r0:r1, :]
        qa = qa_ref[r0:r1, :]
        qboth = jnp.concatenate(
            [jnp.concatenate([q * half, qa * aug], axis=1) for half, aug in head_masks[pp]], axis=0)
        s_diag = jnp.where(visible, _nt_dot(kcat_ref[pp, r0:r1, :], qboth), NEG_BIG)
        s_past = _nt_dot(kcat_ref[pp, :r0, :], qboth) if i > 0 else None
        if use_row_max:
            m = jnp.max(s_diag, axis=0, keepdims=True)
            if i > 0:
                m = jnp.maximum(m, jnp.max(s_past, axis=0, keepdims=True))
                s_past = s_past - m
            s_diag = s_diag - m
        p_past = jnp.exp2(s_past).astype(BF16) if i > 0 else None
        return p_past, jnp.exp2(s_diag).astype(BF16)

    def finish(pp, i, p_past, p_diag):
        r0, r1 = i * tq, (i + 1) * tq
        acc = _dot(vt_ref[pp, :, r0:r1], p_diag)
        if i > 0:
            acc = acc + _dot(vt_ref[pp, :, :r0], p_past)
        inv = 1.0 / acc[LANES:LANES + 1, :]
        o = jnp.concatenate([acc[:HEAD_DIM, :tq] * inv[:, :tq],
                             acc[HEAD_DIM:LANES, tq:] * inv[:, tq:]], axis=0).T
        o_ref[pp, r0:r1, :] = (o * gate_ref[pp, r0:r1, :].astype(F32)).astype(BF16)

    order = [(pp, i) for pp in range(ATT_PAIRS) for i in range(seq // tq - 1, -1, -1)]
    pending = [scores(*blk) for blk in order[:ATT_AHEAD]]
    for n, blk in enumerate(order):
        if n + ATT_AHEAD < len(order):
            pending.append(scores(*order[n + ATT_AHEAD]))
        finish(*blk, *pending.pop(0))


def _fox_attn(q, k, v, gate, qa, ka, batch, seq, use_row_max):
    pairs, t, _ = q.shape
    pair_spec = pl.BlockSpec((ATT_PAIRS, seq, LANES), lambda b, p: (p, b, 0))
    aug_spec = pl.BlockSpec((seq, LANES), lambda b, p: (b, 0))
    return pl.pallas_call(
        functools.partial(_fox_attn_kernel, use_row_max=use_row_max, shift_lo=2 * pairs * AUG_PER_HEAD),
        grid=(batch, pairs // ATT_PAIRS),
        in_specs=[pair_spec, aug_spec, pair_spec, aug_spec, pair_spec, pair_spec],
        out_specs=pair_spec,
        out_shape=jax.ShapeDtypeStruct((pairs, t, LANES), BF16),
        scratch_shapes=[pltpu.VMEM((ATT_PAIRS, seq, 2 * LANES), BF16),
                        pltpu.VMEM((ATT_PAIRS, LANES + ONES_ROWS, seq), BF16)],
        compiler_params=_compiler_params(2),
        name="fox_attn_rowmax" if use_row_max else "fox_attn",
    )(q, qa, k, ka, v, gate)


def kernel(x, gmlp_w_in, gmlp_ln_g, gmlp_ln_b, gmlp_w_s, gmlp_b_s, gmlp_w_out, fox_w_in, fox_b_f, fox_q_g, fox_k_g, fox_w_out, mix_norm_g, mlp_norm_g, mlp_w1, mlp_w2):
    batch, seq, d = x.shape
    depth = mix_norm_g.shape[0]
    wdt = fox_w_out.shape[1]
    n_heads = wdt // HEAD_DIM
    xs = x.reshape(batch * seq, d)

    gmlp_w_in, gmlp_w_out, fox_w_in_bf, fox_w_out = (
        w.astype(BF16) for w in (gmlp_w_in, gmlp_w_out, fox_w_in, fox_w_out))
    fox_layers = list(range(1, depth, 2))
    mlp_w1_fox = _to_bf16_layers(mlp_w1, fox_layers)
    mlp_w2_fox = _to_bf16_layers(mlp_w2, fox_layers)

    for layer in range(depth):
        j = layer // 2
        if layer % 2 == 0:
            xs = _gmlp_layer(xs, mix_norm_g[layer], gmlp_w_in, gmlp_ln_g[j], gmlp_ln_b[j],
                             gmlp_w_s, gmlp_b_s[j], gmlp_w_out, j)
            xs = _mlp_layer_f32_weights(xs, mlp_norm_g[layer], mlp_w1, mlp_w2, layer)
        else:
            w_f = jnp.pad(fox_w_in[j, :, 4 * wdt:], ((0, 0), (0, LANES - n_heads))).astype(BF16)
            b_f = jnp.pad(fox_b_f[j], (0, LANES - n_heads)).reshape(1, LANES)
            q_gain = (jnp.tile(fox_q_g[j], n_heads) * (HEAD_DIM ** -0.5 * LOG2E)).reshape(1, wdt)
            k_gain = jnp.tile(fox_k_g[j], n_heads).reshape(1, wdt)
            bound = (jnp.max(jnp.abs(fox_q_g[j])) * jnp.max(jnp.abs(fox_k_g[j]))) * (HEAD_DIM * HEAD_DIM ** -0.5 * LOG2E)
            fixed_shift_ok = bound <= ATT_MAX_FIXED_SHIFT
            q, k, v, gate, qa, ka = _fox_proj(xs, mix_norm_g[layer], fox_w_in_bf, j, w_f, b_f,
                                              q_gain, k_gain, seq, jnp.where(fixed_shift_ok, bound, 0.0))
            og = lax.cond(fixed_shift_ok,
                          functools.partial(_fox_attn, batch=batch, seq=seq, use_row_max=False),
                          functools.partial(_fox_attn, batch=batch, seq=seq, use_row_max=True),
                          q, k, v, gate, qa, ka)
            xs = _mlp_layer_fused(xs, mlp_norm_g[layer], mlp_w1_fox, mlp_w2_fox, og, fox_w_out, j)
    return xs.reshape(batch, seq, d)
```

```python
import functools
import math

import numpy as np
import jax
import jax.numpy as jnp
from jax import lax
from jax.experimental import pallas as pl
from jax.experimental.pallas import tpu as pltpu

F32 = jnp.float32
BF16 = jnp.bfloat16

CHUNK = 128
GMLP_GROUPS = 8
HEAD_DIM = 64
RMS_EPS = 1e-6
LN_EPS = 1e-5
LOG2E = math.log2(math.e)

LANES = 128
MXU_DIM = 256
VMEM_LIMIT_BYTES = 56 * 1024 * 1024

MLP_ROWS = 1024
MLP_FF_TILE = 512
CONVERT_BLOCK_BYTES = 8 * 1024 * 1024
GMLP_ROWS = 1024
PROJ_ROWS = 1024
CUMSUM_ROWS = 512
PROJ_SLAB = 512
ATT_Q = 256
ATT_PAIRS = 2
ATT_AHEAD = 4
ONES_ROWS = 16
AUG_PER_HEAD = 6
SHIFT_LANES = 3
ATT_MAX_FIXED_SHIFT = 48.0
NEG_BIG = -1e30


def _compiler_params(n_axes):
    return pltpu.CompilerParams(
        dimension_semantics=("arbitrary",) * n_axes,
        vmem_limit_bytes=VMEM_LIMIT_BYTES,
    )


def _resident(shape):
    nd = len(shape)
    return pl.BlockSpec(shape, lambda *_: (0,) * nd, pipeline_mode=pl.Buffered(1))


def _layer_resident(stacked_shape, layer, width=None):
    tail = tuple(stacked_shape[1:-1]) + (width or stacked_shape[-1],)
    return pl.BlockSpec((None,) + tail, lambda *_: (layer,) + (0,) * len(tail),
                        pipeline_mode=pl.Buffered(1))


def _rms_norm(x, g):
    ms = jnp.mean(x * x, axis=-1, keepdims=True)
    return x * lax.rsqrt(ms + RMS_EPS) * g


def _gelu_tanh(x):
    c = math.sqrt(2.0 / math.pi)
    return x * (0.5 * (1.0 + jnp.tanh(c * (x + 0.044715 * (x * x * x)))))


def _dot(a, b):
    return jnp.dot(a, b, preferred_element_type=F32)


def _mlp_fused_kernel(x_ref, og_ref, wo_ref, g_ref, w1_ref, w2_ref, o_ref):
    tf = MLP_FF_TILE
    og = jnp.concatenate([og_ref[p] for p in range(og_ref.shape[0])], axis=1)
    x = x_ref[...] + _dot(og, wo_ref[...])
    h = _rms_norm(x, g_ref[...]).astype(BF16)
    acc = x
    for f0 in range(0, w1_ref.shape[1], tf):
        a = jnp.maximum(_dot(h, w1_ref[:, f0:f0 + tf]), 0.0)
        acc = acc + _dot((a * a).astype(BF16), w2_ref[f0:f0 + tf, :])
    o_ref[...] = acc


def _mlp_layer_fused(x, g, w1, w2, og, wo, layer):
    t, d = x.shape
    tm = MLP_ROWS
    row_spec = pl.BlockSpec((tm, d), lambda i: (i, 0))
    return pl.pallas_call(
        _mlp_fused_kernel,
        grid=(t // tm,),
        in_specs=[row_spec,
                  pl.BlockSpec((og.shape[0], tm, LANES), lambda i: (0, i, 0)),
                  _layer_resident(wo.shape, layer),
                  _resident((1, d)),
                  _layer_resident(w1.shape, layer),
                  _layer_resident(w2.shape, layer)],
        out_specs=row_spec,
        out_shape=jax.ShapeDtypeStruct((t, d), F32),
        compiler_params=_compiler_params(1),
        name="mlp_fused",
    )(x, og, wo, g.reshape(1, d), w1, w2)


def _mlp_stream_kernel(x_ref, g_ref, w1f_ref, w2f_ref, o_ref, w1_ref, w2_ref, h_ref, *, n_chunks):
    s = pl.program_id(0)

    @pl.when(s < n_chunks)
    def _():
        w1c = w1f_ref[...].astype(BF16)
        w2c = w2f_ref[...].astype(BF16)
        w1_ref[s] = w1c
        w2_ref[s] = w2c

        @pl.when(s == 0)
        def _():
            x = x_ref[...]
            h_ref[...] = _rms_norm(x, g_ref[...]).astype(BF16)
            o_ref[...] = x

        a = jnp.maximum(_dot(h_ref[...], w1c), 0.0)
        o_ref[...] += _dot((a * a).astype(BF16), w2c)

    @pl.when(s >= n_chunks)
    def _():
        x = x_ref[...]
        h = _rms_norm(x, g_ref[...]).astype(BF16)
        acc = x
        for j in range(n_chunks):
            a = jnp.maximum(_dot(h, w1_ref[j]), 0.0)
            acc = acc + _dot((a * a).astype(BF16), w2_ref[j])
        o_ref[...] = acc


def _convert_kernel(src_ref, dst_ref):
    dst_ref[...] = src_ref[...].astype(dst_ref.dtype)


def _to_bf16_layers(w, layers):
    _, rows, cols = w.shape
    col_block = min(cols, CONVERT_BLOCK_BYTES // (rows * w.dtype.itemsize))
    first, step = layers[0], layers[1] - layers[0]
    assert list(layers) == [first + step * n for n in range(len(layers))]
    return pl.pallas_call(
        _convert_kernel,
        grid=(len(layers), cols // col_block),
        in_specs=[pl.BlockSpec((None, rows, col_block), lambda n, j: (first + step * n, 0, j))],
        out_specs=pl.BlockSpec((None, rows, col_block), lambda n, j: (n, 0, j)),
        out_shape=jax.ShapeDtypeStruct((len(layers), rows, cols), BF16),
        compiler_params=_compiler_params(2),
        name="to_bf16",
    )(w)


def _mlp_layer_f32_weights(x, g, w1, w2, layer):
    t, d = x.shape
    ff = w1.shape[2]
    tm, tf = MLP_ROWS, MLP_FF_TILE
    n_chunks = ff // tf
    last = n_chunks - 1
    row_spec = pl.BlockSpec((tm, d), lambda s: (jnp.maximum(s - last, 0), 0))
    return pl.pallas_call(
        functools.partial(_mlp_stream_kernel, n_chunks=n_chunks),
        grid=(n_chunks + t // tm - 1,),
        in_specs=[
            row_spec,
            _resident((1, d)),
            pl.BlockSpec((None, d, tf), lambda s: (layer, 0, jnp.minimum(s, last))),
            pl.BlockSpec((None, tf, d), lambda s: (layer, jnp.minimum(s, last), 0)),
        ],
        out_specs=row_spec,
        out_shape=jax.ShapeDtypeStruct((t, d), F32),
        scratch_shapes=[pltpu.VMEM((n_chunks, d, tf), BF16), pltpu.VMEM((n_chunks, tf, d), BF16),
                        pltpu.VMEM((tm, d), BF16)],
        compiler_params=_compiler_params(1),
        name="mlp_f32w",
    )(x, g.reshape(1, d), w1, w2)


def _gmlp_kernel(x_ref, g_ref, win_ref, lng_ref, lnb_ref, ws_ref, bst_ref, wout_ref,
                 o_ref, vn_ref, gated_ref):
    tm = x_ref.shape[0]
    e = lng_ref.shape[1]
    gd = e // GMLP_GROUPS
    x = x_ref[...]
    h = _rms_norm(x, g_ref[...]).astype(BF16)

    v = _gelu_tanh(_dot(h, win_ref[:, e:]))
    u_all = _gelu_tanh(_dot(h, win_ref[:, :e]))
    mu = jnp.mean(v, axis=-1, keepdims=True)
    var = jnp.mean(v * v, axis=-1, keepdims=True) - mu * mu
    vn_ref[...] = ((v - mu) * lax.rsqrt(var + LN_EPS) * lng_ref[...] + lnb_ref[...]).astype(BF16)

    row = lax.broadcasted_iota(jnp.int32, (CHUNK, CHUNK), 0)
    col = lax.broadcasted_iota(jnp.int32, (CHUNK, CHUNK), 1)
    causal = row >= col
    for grp in range(GMLP_GROUPS):
        cols = slice(grp * gd, (grp + 1) * gd)
        w = jnp.where(causal, ws_ref[grp], 0.0).astype(BF16)
        bias = bst_ref[:, grp:grp + 1]
        for c in range(tm // CHUNK):
            rows = slice(c * CHUNK, (c + 1) * CHUNK)
            s = _dot(w, vn_ref[rows, cols]) + bias
            gated_ref[rows, cols] = (u_all[rows, cols] * s).astype(BF16)

    o_ref[...] = x + _dot(gated_ref[...], wout_ref[...])


def _gmlp_layer(x, g, w_in, ln_g, ln_b, w_s, b_s, w_out, layer):
    t, d = x.shape
    e = ln_g.shape[0]
    tm = GMLP_ROWS
    return pl.pallas_call(
        _gmlp_kernel,
        grid=(t // tm,),
        in_specs=[
            pl.BlockSpec((tm, d), lambda i: (i, 0)),
            _resident((1, d)),
            _layer_resident(w_in.shape, layer),
            _resident((1, e)),
            _resident((1, e)),
            _layer_resident(w_s.shape, layer),
            _resident((CHUNK, GMLP_GROUPS)),
            _layer_resident(w_out.shape, layer),
        ],
        out_specs=pl.BlockSpec((tm, d), lambda i: (i, 0)),
        out_shape=jax.ShapeDtypeStruct((t, d), F32),
        scratch_shapes=[pltpu.VMEM((tm, e), BF16), pltpu.VMEM((tm, e), BF16)],
        compiler_params=_compiler_params(1),
        name="gmlp",
    )(x, g.reshape(1, d), w_in, ln_g.reshape(1, e), ln_b.reshape(1, e), w_s, b_s.T, w_out)


def _split3(x):
    hi = x.astype(BF16)
    r = x - hi.astype(F32)
    mid = r.astype(BF16)
    lo = (r - mid.astype(F32)).astype(BF16)
    return hi, mid, lo


def _store_pairs(dst_ref, col0, val):
    for c in range(0, val.shape[1], LANES):
        dst_ref[(col0 + c) // LANES] = val[:, c:c + LANES]


def _fox_proj_kernel(x_ref, g_ref, w_ref, wf_ref, bf_ref, qg_ref, kg_ref, gmat_ref,
                     eqk_ref, oneq_ref, onek_ref,
                     q_ref, k_ref, v_ref, gate_ref, qa_ref, ka_ref, carry_ref,
                     *, blocks_per_seq):
    tm = x_ref.shape[0]
    wdt = qg_ref.shape[1]
    i = pl.program_id(0)
    h = _rms_norm(x_ref[...], g_ref[...]).astype(BF16)

    gmat = gmat_ref[...]

    def normed_heads(base, gain_ref, dst):
        for s0 in range(0, wdt, PROJ_SLAB):
            y = _dot(h, w_ref[:, base + s0: base + s0 + PROJ_SLAB])
            for c0 in range(0, PROJ_SLAB, MXU_DIM):
                cols = slice(s0 + c0, s0 + c0 + MXU_DIM)
                yy = y[:, c0:c0 + MXU_DIM]
                ms = _dot((yy * yy).astype(BF16), gmat)
                _store_pairs(dst, s0 + c0, (yy * lax.rsqrt(ms + RMS_EPS) * gain_ref[:, cols]).astype(BF16))

    f = _dot(h, wf_ref[...]) + bf_ref[...]
    logf = jnp.minimum(f, 0.0) - jnp.log(1.0 + jnp.exp(-jnp.abs(f)))
    normed_heads(0, qg_ref, q_ref)

    cr = min(tm, CUMSUM_ROWS)
    row = lax.broadcasted_iota(jnp.int32, (cr, cr), 0)
    col = lax.broadcasted_iota(jnp.int32, (cr, cr), 1)
    tril = jnp.where(row >= col, 1.0, 0.0).astype(BF16)
    first = (i % blocks_per_seq) == 0
    carry = jnp.where(first, 0.0, carry_ref[0:1, :])
    c_parts = []
    for r0 in range(0, tm, cr):
        csum = _dot(tril, jnp.concatenate(_split3(logf[r0:r0 + cr]), axis=1))
        part = (csum[:, :LANES] + csum[:, LANES:2 * LANES]) + csum[:, 2 * LANES:] + carry
        carry = part[cr - 1:cr, :]
        c_parts.append(part)
    c = jnp.concatenate(c_parts, axis=0)
    carry_ref[0:1, :] = carry
    normed_heads(wdt, kg_ref, k_ref)

    pieces = jnp.concatenate(_split3(c * LOG2E), axis=1)
    slabs = _dot(pieces, eqk_ref[...])
    qa_ref[...] = (slabs[:, :LANES] + oneq_ref[...]).astype(BF16)
    ka_ref[...] = (onek_ref[...] - slabs[:, LANES:]).astype(BF16)

    for s0 in range(0, wdt, PROJ_SLAB):
        _store_pairs(v_ref, s0, _dot(h, w_ref[:, 2 * wdt + s0: 2 * wdt + s0 + PROJ_SLAB]).astype(BF16))
        gt = _dot(h, w_ref[:, 3 * wdt + s0: 3 * wdt + s0 + PROJ_SLAB])
        _store_pairs(gate_ref, s0, (1.0 / (1.0 + jnp.exp(-gt))).astype(BF16))


def _bias_slab_constants(n_heads):
    eqk = np.zeros((3 * LANES, 2 * LANES), np.float32)
    oneq = np.zeros((1, LANES), np.float32)
    onek = np.zeros((1, LANES), np.float32)
    for h in range(n_heads):
        for p in range(3):
            eqk[p * LANES + h, AUG_PER_HEAD * h + p] = 1.0
            eqk[p * LANES + h, LANES + AUG_PER_HEAD * h + 3 + p] = 1.0
            oneq[0, AUG_PER_HEAD * h + 3 + p] = 1.0
            onek[0, AUG_PER_HEAD * h + p] = 1.0
    onek[0, AUG_PER_HEAD * n_heads: AUG_PER_HEAD * n_heads + SHIFT_LANES] = 1.0
    return jnp.asarray(eqk, BF16), jnp.asarray(oneq), jnp.asarray(onek)


def _head_mean_matrix():
    idx = np.arange(MXU_DIM) // HEAD_DIM
    return jnp.asarray((idx[:, None] == idx[None, :]).astype(np.float32) / HEAD_DIM, BF16)


def _fox_proj(x, g, w_in, layer, w_f, b_f, q_gain, k_gain, seq, logit_shift):
    t, d = x.shape
    wdt = q_gain.shape[1]
    n_heads = wdt // HEAD_DIM
    tm = PROJ_ROWS
    eqk, oneq, onek = _bias_slab_constants(n_heads)
    shift_pieces = jnp.stack([p.astype(F32) for p in _split3(-logit_shift)])
    oneq = lax.dynamic_update_slice(oneq, shift_pieces.reshape(1, SHIFT_LANES), (0, AUG_PER_HEAD * n_heads))
    gmat = _head_mean_matrix()
    row_spec = lambda width: pl.BlockSpec((tm, width), lambda i: (i, 0))
    bf = lambda width: jax.ShapeDtypeStruct((t, width), BF16)
    pairs = wdt // LANES
    pair_spec = pl.BlockSpec((pairs, tm, LANES), lambda i: (0, i, 0))
    pair_shape = jax.ShapeDtypeStruct((pairs, t, LANES), BF16)
    return pl.pallas_call(
        functools.partial(_fox_proj_kernel, blocks_per_seq=seq // tm),
        grid=(t // tm,),
        in_specs=[
            row_spec(d), _resident((1, d)), _layer_resident(w_in.shape, layer, width=4 * wdt),
            _resident(w_f.shape),
            _resident((1, LANES)), _resident((1, wdt)), _resident((1, wdt)),
            _resident(gmat.shape), _resident(eqk.shape),
            _resident((1, LANES)), _resident((1, LANES)),
        ],
        out_specs=[pair_spec] * 4 + [row_spec(LANES), row_spec(LANES)],
        out_shape=[pair_shape] * 4 + [bf(LANES), bf(LANES)],
        scratch_shapes=[pltpu.VMEM((8, LANES), F32)],
        compiler_params=_compiler_params(1),
        name="fox_proj",
    )(x, g.reshape(1, d), w_in, w_f, b_f, q_gain, k_gain, gmat, eqk, oneq, onek)


def _nt_dot(a, b):
    return lax.dot_general(a, b, (((1,), (1,)), ((), ())), preferred_element_type=F32)


def _fox_attn_kernel(q_ref, qa_ref, k_ref, ka_ref, v_ref, gate_ref, o_ref, kcat_ref, vt_ref,
                     *, use_row_max, shift_lo):
    seq = k_ref.shape[1]
    tq = ATT_Q
    group = pl.program_id(1)
    lane = lax.broadcasted_iota(jnp.int32, (1, LANES), 1)

    key_idx = lax.broadcasted_iota(jnp.int32, (tq, 2 * tq), 0)
    qry_idx = lax.broadcasted_iota(jnp.int32, (tq, 2 * tq), 1)
    visible = key_idx <= qry_idx - jnp.where(qry_idx >= tq, tq, 0)

    head_masks = []
    for pp in range(ATT_PAIRS):
        kcat_ref[pp, :, :LANES] = k_ref[pp]
        kcat_ref[pp, :, LANES:] = ka_ref[...]
        vt_ref[pp, :LANES, :] = v_ref[pp].astype(F32).T.astype(BF16)
        vt_ref[pp, LANES:, :] = jnp.ones((ONES_ROWS, seq), BF16)
        masks = []
        for e in range(2):
            in_half = (lane >= HEAD_DIM * e) & (lane < HEAD_DIM * (e + 1))
            aug_lo = AUG_PER_HEAD * (2 * (ATT_PAIRS * group + pp) + e)
            in_aug = ((lane >= aug_lo) & (lane < aug_lo + AUG_PER_HEAD)) | (
                (lane >= shift_lo) & (lane < shift_lo + SHIFT_LANES))
            masks.append((jnp.where(in_half, 1.0, 0.0).astype(BF16),
                          jnp.where(in_aug, 1.0, 0.0).astype(BF16)))
        head_masks.append(masks)

    def scores(pp, i):
        r0, r1 = i * tq, (i + 1) * tq
        q = q_ref[pp, r0:r1, :]
        qa = qa_ref[r0:r1, :]
        qboth = jnp.concatenate(
            [jnp.concatenate([q * half, qa * aug], axis=1) for half, aug in head_masks[pp]], axis=0)
        s_diag = jnp.where(visible, _nt_dot(kcat_ref[pp, r0:r1, :], qboth), NEG_BIG)
        s_past = _nt_dot(kcat_ref[pp, :r0, :], qboth) if i > 0 else None
        if use_row_max:
            m = jnp.max(s_diag, axis=0, keepdims=True)
            if i > 0:
                m = jnp.maximum(m, jnp.max(s_past, axis=0, keepdims=True))
                s_past = s_past - m
            s_diag = s_diag - m
        p_past = jnp.exp2(s_past).astype(BF16) if i > 0 else None
        return p_past, jnp.exp2(s_diag).astype(BF16)

    def finish(pp, i, p_past, p_diag):
        r0, r1 = i * tq, (i + 1) * tq
        acc = _dot(vt_ref[pp, :, r0:r1], p_diag)
        if i > 0:
            acc = acc + _dot(vt_ref[pp, :, :r0], p_past)
        inv = 1.0 / acc[LANES:LANES + 1, :]
        o = jnp.concatenate([acc[:HEAD_DIM, :tq] * inv[:, :tq],
                             acc[HEAD_DIM:LANES, tq:] * inv[:, tq:]], axis=0).T
        o_ref[pp, r0:r1, :] = (o * gate_ref[pp, r0:r1, :].astype(F32)).astype(BF16)

    order = [(pp, i) for pp in range(ATT_PAIRS) for i in range(seq // tq - 1, -1, -1)]
    pending = [scores(*blk) for blk in order[:ATT_AHEAD]]
    for n, blk in enumerate(order):
        if n + ATT_AHEAD < len(order):
            pending.append(scores(*order[n + ATT_AHEAD]))
        finish(*blk, *pending.pop(0))


def _fox_attn(q, k, v, gate, qa, ka, batch, seq, use_row_max):
    pairs, t, _ = q.shape
    pair_spec = pl.BlockSpec((ATT_PAIRS, seq, LANES), lambda b, p: (p, b, 0))
    aug_spec = pl.BlockSpec((seq, LANES), lambda b, p: (b, 0))
    return pl.pallas_call(
        functools.partial(_fox_attn_kernel, use_row_max=use_row_max, shift_lo=2 * pairs * AUG_PER_HEAD),
        grid=(batch, pairs // ATT_PAIRS),
        in_specs=[pair_spec, aug_spec, pair_spec, aug_spec, pair_spec, pair_spec],
        out_specs=pair_spec,
        out_shape=jax.ShapeDtypeStruct((pairs, t, LANES), BF16),
        scratch_shapes=[pltpu.VMEM((ATT_PAIRS, seq, 2 * LANES), BF16),
                        pltpu.VMEM((ATT_PAIRS, LANES + ONES_ROWS, seq), BF16)],
        compiler_params=_compiler_params(2),
        name="fox_attn_rowmax" if use_row_max else "fox_attn",
    )(q, qa, k, ka, v, gate)


def kernel(x, gmlp_w_in, gmlp_ln_g, gmlp_ln_b, gmlp_w_s, gmlp_b_s, gmlp_w_out, fox_w_in, fox_b_f, fox_q_g, fox_k_g, fox_w_out, mix_norm_g, mlp_norm_g, mlp_w1, mlp_w2):
    batch, seq, d = x.shape
    depth = mix_norm_g.shape[0]
    wdt = fox_w_out.shape[1]
    n_heads = wdt // HEAD_DIM
    xs = x.reshape(batch * seq, d)

    gmlp_w_in, gmlp_w_out, fox_w_in_bf, fox_w_out = (
        w.astype(BF16) for w in (gmlp_w_in, gmlp_w_out, fox_w_in, fox_w_out))
    fox_layers = list(range(1, depth, 2))
    mlp_w1_fox = _to_bf16_layers(mlp_w1, fox_layers)
    mlp_w2_fox = _to_bf16_layers(mlp_w2, fox_layers)

    for layer in range(depth):
        j = layer // 2
        if layer % 2 == 0:
            xs = _gmlp_layer(xs, mix_norm_g[layer], gmlp_w_in, gmlp_ln_g[j], gmlp_ln_b[j],
                             gmlp_w_s, gmlp_b_s[j], gmlp_w_out, j)
            xs = _mlp_layer_f32_weights(xs, mlp_norm_g[layer], mlp_w1, mlp_w2, layer)
        else:
            w_f = jnp.pad(fox_w_in[j, :, 4 * wdt:], ((0, 0), (0, LANES - n_heads))).astype(BF16)
            b_f = jnp.pad(fox_b_f[j], (0, LANES - n_heads)).reshape(1, LANES)
            q_gain = (jnp.tile(fox_q_g[j], n_heads) * (HEAD_DIM ** -0.5 * LOG2E)).reshape(1, wdt)
            k_gain = jnp.tile(fox_k_g[j], n_heads).reshape(1, wdt)
            bound = (jnp.max(jnp.abs(fox_q_g[j])) * jnp.max(jnp.abs(fox_k_g[j]))) * (HEAD_DIM * HEAD_DIM ** -0.5 * LOG2E)
            fixed_shift_ok = bound <= ATT_MAX_FIXED_SHIFT
            q, k, v, gate, qa, ka = _fox_proj(xs, mix_norm_g[layer], fox_w_in_bf, j, w_f, b_f,
                                              q_gain, k_gain, seq, jnp.where(fixed_shift_ok, bound, 0.0))
            og = lax.cond(fixed_shift_ok,
                          functools.partial(_fox_attn, batch=batch, seq=seq, use_row_max=False),
                          functools.partial(_fox_attn, batch=batch, seq=seq, use_row_max=True),
                          q, k, v, gate, qa, ka)
            xs = _mlp_layer_fused(xs, mlp_norm_g[layer], mlp_w1_fox, mlp_w2_fox, og, fox_w_out, j)
    return xs.reshape(batch, seq, d)
```
